```python
import math
import jax, jax.numpy as jnp
from jax import lax
import numpy as np

D_MODEL = 1024
BATCH = 8
SEQ = 2048
DEPTH = 4

N_MIXERS = 2
EXPAND = 2
D_INNER = EXPAND * D_MODEL
SSD_HEAD_DIM = 64
SSD_HEADS = D_INNER // SSD_HEAD_DIM
SSD_GROUPS = 8
SSD_STATE = 128
SSD_CONV = 5
SSD_CHUNK = 128
CONV_DIM = D_INNER + 2 * SSD_GROUPS * SSD_STATE
SSD_IN = D_INNER + CONV_DIM + 2 * SSD_HEADS
POOL_WINDOWS = (2, 4, 8, 16)
POOL_GROUPS = len(POOL_WINDOWS)
POOL_GROUP_DIM = D_INNER // POOL_GROUPS
POOL_IN = 2 * D_INNER
EPS = 1e-6
DT_MIN = 1e-3
DT_MAX = 1e-1

kernel_name = "bidir_ssd_pool_hybrid"


def rmsnorm(x, w):
    xf = x.astype(jnp.float32)
    y = xf * lax.rsqrt(jnp.mean(xf * xf, axis=-1, keepdims=True) + EPS)
    return (y * w.astype(jnp.float32)).astype(x.dtype)


def segsum(a):
    t = a.shape[-1]
    cs = jnp.cumsum(a, axis=-1)
    diff = cs[..., :, None] - cs[..., None, :]
    mask = jnp.tril(jnp.ones((t, t), dtype=bool))
    return jnp.where(mask, diff, -jnp.inf)


def ssd_scan(xs, a, bm, cm):
    b, l, h, p = xs.shape
    g, n = bm.shape[2], bm.shape[3]
    r = h // g
    q = SSD_CHUNK
    c = l // q
    xc = xs.reshape(b, c, q, g, r, p)
    ac = a.reshape(b, c, q, g, r).transpose(0, 3, 4, 1, 2)
    bc = bm.reshape(b, c, q, g, n)
    cc = cm.reshape(b, c, q, g, n)
    a_cs = jnp.cumsum(ac, axis=-1)
    lmat = jnp.exp(segsum(ac))
    cb = jnp.einsum('bclgn,bcsgn->bgcls', cc, bc)
    y_diag = jnp.einsum('bgrcls,bcsgrp->bclgrp', cb[:, :, None] * lmat, xc)
    decay_states = jnp.exp(a_cs[..., -1:] - a_cs)
    states = jnp.einsum('bclgn,bgrcl,bclgrp->bcgrpn', bc, decay_states, xc)
    chunk_decay = jnp.exp(a_cs[..., -1])

    def step(hstate, inp):
        s, dcy = inp
        return dcy[..., None, None] * hstate + s, hstate

    h0 = jnp.zeros((b, g, r, p, n), dtype=xs.dtype)
    _, prev = lax.scan(step, h0, (states.transpose(1, 0, 2, 3, 4, 5),
                                  chunk_decay.transpose(3, 0, 1, 2)))
    prev = prev.transpose(1, 0, 2, 3, 4, 5)
    y_off = jnp.einsum('bclgn,bcgrpn,bgrcl->bclgrp', cc, prev, jnp.exp(a_cs))
    return (y_diag + y_off).reshape(b, l, h, p)


def depthwise_conv_centred(x, w, bias):
    ch = x.shape[-1]
    k = w.shape[0]
    y = lax.conv_general_dilated(x, w[:, None, :].astype(x.dtype), window_strides=(1,),
                                 padding=[(k // 2, k // 2)],
                                 dimension_numbers=('NWC', 'WIO', 'NWC'),
                                 feature_group_count=ch)
    return y + bias.astype(x.dtype)


def ssd_mixer(u, w_in, conv_w, conv_b, dt_bias, a_log, d_skip, norm_w, w_out):
    b, l, _ = u.shape
    proj = u @ w_in
    z = proj[..., :D_INNER]
    xbc = proj[..., D_INNER:D_INNER + CONV_DIM]
    dt_raw = proj[..., D_INNER + CONV_DIM:]
    xbc = jax.nn.silu(depthwise_conv_centred(xbc, conv_w, conv_b))
    gn = SSD_GROUPS * SSD_STATE
    xs = xbc[..., :D_INNER].astype(jnp.float32).reshape(b, l, SSD_HEADS, SSD_HEAD_DIM)
    bm = xbc[..., D_INNER:D_INNER + gn].astype(jnp.float32).reshape(b, l, SSD_GROUPS, SSD_STATE)
    cm = xbc[..., D_INNER + gn:].astype(jnp.float32).reshape(b, l, SSD_GROUPS, SSD_STATE)
    dt = jax.nn.softplus(dt_raw.astype(jnp.float32).reshape(b, l, 2, SSD_HEADS)
                         + dt_bias.astype(jnp.float32))
    a = -jnp.exp(a_log.astype(jnp.float32))
    y = xs * d_skip.astype(jnp.float32)[:, None]
    for direction in range(2):
        xd = xs * dt[:, :, direction, :, None]
        ad = dt[:, :, direction] * a[direction]
        if direction == 1:
            yd = ssd_scan(jnp.flip(xd, 1), jnp.flip(ad, 1), jnp.flip(bm, 1), jnp.flip(cm, 1))
            yd = jnp.flip(yd, 1)
        else:
            yd = ssd_scan(xd, ad, bm, cm)
        y = y + yd
    y = y.reshape(b, l, D_INNER) * jax.nn.silu(z.astype(jnp.float32))
    yg = y.reshape(b, l, SSD_GROUPS, D_INNER // SSD_GROUPS)
    yg = yg * lax.rsqrt(jnp.mean(yg * yg, axis=-1, keepdims=True) + EPS)
    y = (yg.reshape(b, l, D_INNER) * norm_w.astype(jnp.float32)).astype(u.dtype)
    return y @ w_out


def pool_mixer(u, w_in, mix_w, scale, w_out):
    b, l, _ = u.shape
    proj = u @ w_in
    v = proj[..., :D_INNER]
    gate = proj[..., D_INNER:]
    vf = v.astype(jnp.float32)
    cs = jnp.concatenate([jnp.zeros((b, 1, D_INNER), jnp.float32),
                          jnp.cumsum(vf, axis=1)], axis=1)
    pos = jnp.arange(l)
    outs = []
    for gi, w in enumerate(POOL_WINDOWS):
        lo = jnp.clip(pos - w // 2, 0, l)
        hi = jnp.clip(pos + w - w // 2, 0, l)
        seg = cs[:, :, gi * POOL_GROUP_DIM:(gi + 1) * POOL_GROUP_DIM]
        s = jnp.take(seg, hi, axis=1) - jnp.take(seg, lo, axis=1)
        cnt = (hi - lo).astype(jnp.float32)
        outs.append(s / cnt[None, :, None])
    pooled = jnp.concatenate(outs, axis=-1) - vf
    pooled = pooled.reshape(b, l, POOL_GROUPS, POOL_GROUP_DIM)
    mixed = jnp.einsum('blgc,gcd->blgd', pooled, mix_w.astype(jnp.float32)).reshape(b, l, D_INNER)
    y = mixed * scale.astype(jnp.float32) * jax.nn.silu(gate.astype(jnp.float32))
    return y.astype(u.dtype) @ w_out


def setup_inputs(seed: int = 0) -> dict:
    key = jax.random.key(seed)
    ks = jax.random.split(key, 20)
    n_ssd = len(range(0, DEPTH, N_MIXERS))
    n_pool = DEPTH - n_ssd
    f32 = jnp.float32
    x = jax.random.normal(ks[0], (BATCH, SEQ, D_MODEL), f32)
    norm_w = 1.0 + 0.02 * jax.random.normal(ks[1], (DEPTH, D_MODEL), f32)
    ssd_w_in = jax.random.normal(ks[2], (n_ssd, D_MODEL, SSD_IN), f32) * D_MODEL ** -0.5
    ssd_conv_w = jax.random.normal(ks[3], (n_ssd, SSD_CONV, CONV_DIM), f32) * SSD_CONV ** -0.5
    ssd_conv_b = 0.02 * jax.random.normal(ks[4], (n_ssd, CONV_DIM), f32)
    dt0 = jnp.exp(jax.random.uniform(ks[5], (n_ssd, 2, SSD_HEADS), f32,
                                     math.log(DT_MIN), math.log(DT_MAX)))
    ssd_dt_bias = dt0 + jnp.log(-jnp.expm1(-dt0))
    ssd_a_log = jnp.log(jax.random.uniform(ks[6], (n_ssd, 2, SSD_HEADS), f32, 1.0, 16.0))
    ssd_d = 1.0 + 0.1 * jax.random.normal(ks[7], (n_ssd, SSD_HEADS), f32)
    ssd_norm_w = 1.0 + 0.02 * jax.random.normal(ks[8], (n_ssd, D_INNER), f32)
    ssd_w_out = jax.random.normal(ks[9], (n_ssd, D_INNER, D_MODEL), f32) * D_INNER ** -0.5
    pool_w_in = jax.random.normal(ks[10], (n_pool, D_MODEL, POOL_IN), f32) * D_MODEL ** -0.5
    pool_mix_w = jax.random.normal(ks[11], (n_pool, POOL_GROUPS, POOL_GROUP_DIM, POOL_GROUP_DIM),
                                   f32) * POOL_GROUP_DIM ** -0.5
    pool_scale = 1.0 + 0.1 * jax.random.normal(ks[12], (n_pool, D_INNER), f32)
    pool_w_out = jax.random.normal(ks[13], (n_pool, D_INNER, D_MODEL), f32) * D_INNER ** -0.5
    final_norm_w = 1.0 + 0.02 * jax.random.normal(ks[14], (D_MODEL,), f32)
    return {"x": x, "norm_w": norm_w,
            "ssd_w_in": ssd_w_in, "ssd_conv_w": ssd_conv_w, "ssd_conv_b": ssd_conv_b,
            "ssd_dt_bias": ssd_dt_bias, "ssd_a_log": ssd_a_log, "ssd_d": ssd_d,
            "ssd_norm_w": ssd_norm_w, "ssd_w_out": ssd_w_out,
            "pool_w_in": pool_w_in, "pool_mix_w": pool_mix_w, "pool_scale": pool_scale,
            "pool_w_out": pool_w_out, "final_norm_w": final_norm_w}


def reference(x, norm_w, ssd_w_in, ssd_conv_w, ssd_conv_b, ssd_dt_bias, ssd_a_log, ssd_d,
              ssd_norm_w, ssd_w_out, pool_w_in, pool_mix_w, pool_scale, pool_w_out,
              final_norm_w):
    h = x
    for i in range(DEPTH):
        u = rmsnorm(h, norm_w[i])
        j = i // N_MIXERS
        if i % N_MIXERS == 0:
            out = ssd_mixer(u, ssd_w_in[j], ssd_conv_w[j], ssd_conv_b[j], ssd_dt_bias[j],
                            ssd_a_log[j], ssd_d[j], ssd_norm_w[j], ssd_w_out[j])
        else:
            out = pool_mixer(u, pool_w_in[j], pool_mix_w[j], pool_scale[j], pool_w_out[j])
        h = h + out
    return rmsnorm(h, final_norm_w)
```

```python
import functools

import jax
import jax.numpy as jnp
from jax import lax
from jax.experimental import pallas as pl
from jax.experimental.pallas import tpu as pltpu

f32 = jnp.float32
bf16 = jnp.bfloat16

EPS = 1e-6
SSD_HEAD_DIM = 64
SSD_GROUPS = 8
SSD_STATE = 128
SSD_CONV = 5
HEADS_PER_GROUP = 4
GROUP_WIDTH = HEADS_PER_GROUP * SSD_HEAD_DIM
CHUNK = 128
HALO = 8
POOL_WINDOWS = (2, 4, 8, 16)
POOL_ROWS = 256

VMEM_LIMIT = 48 * 1024 * 1024


def _silu(v):
    return v / (1.0 + jnp.exp(-v))


def _rows_ahead(win, k):
    n = win.shape[0]
    if k % 8 == 0:
        return win[k:k + n - 2 * HALO, :]
    return pltpu.roll(win, n - k, axis=0)[0:n - 2 * HALO, :]


def _norm_to_scratch(h_ref, nw_ref, u_scr):
    xv = h_ref[...]
    ms = jnp.mean(xv * xv, axis=-1, keepdims=True)
    u_scr[...] = (xv * lax.rsqrt(ms + EPS) * nw_ref[...]).astype(bf16)


def _in_proj_kernel(h_ref, nw_ref, w_ref, o_ref, u_scr):
    @pl.when(pl.program_id(1) == 0)
    def _():
        _norm_to_scratch(h_ref, nw_ref, u_scr)

    o_ref[...] = jnp.dot(u_scr[...], w_ref[...], preferred_element_type=f32).astype(o_ref.dtype)


def _in_proj_dt_kernel(h_ref, nw_ref, w_ref, wdt_ref, o_ref, dt_ref, u_scr):
    @pl.when(pl.program_id(1) == 0)
    def _():
        _norm_to_scratch(h_ref, nw_ref, u_scr)
        dt_ref[...] = lax.dot_general(wdt_ref[...], u_scr[...], (((1,), (1,)), ((), ())),
                                      preferred_element_type=f32)

    o_ref[...] = jnp.dot(u_scr[...], w_ref[...], preferred_element_type=f32).astype(o_ref.dtype)


def _in_proj(h, nw, w, wdt_t=None, *, tm=1024, tn=1024):
    t, d = h.shape
    n = w.shape[1]
    grid = (t // tm, n // tn)
    h_spec = pl.BlockSpec((tm, d), lambda i, j: (i, 0))
    nw_spec = pl.BlockSpec((1, d), lambda i, j: (0, 0))
    w_spec = pl.BlockSpec((d, tn), lambda i, j: (0, j))
    o_spec = pl.BlockSpec((tm, tn), lambda i, j: (i, j))
    params = pltpu.CompilerParams(dimension_semantics=("arbitrary", "arbitrary"),
                                  vmem_limit_bytes=VMEM_LIMIT)
    scratch = [pltpu.VMEM((tm, d), bf16)]
    if wdt_t is None:
        return pl.pallas_call(
            _in_proj_kernel, grid=grid, in_specs=[h_spec, nw_spec, w_spec], out_specs=o_spec,
            out_shape=jax.ShapeDtypeStruct((t, n), bf16), scratch_shapes=scratch,
            compiler_params=params, name="in_proj")(h, nw, w)
    nd = wdt_t.shape[0]
    return pl.pallas_call(
        _in_proj_dt_kernel, grid=grid,
        in_specs=[h_spec, nw_spec, w_spec, pl.BlockSpec((nd, d), lambda i, j: (0, 0))],
        out_specs=[o_spec, pl.BlockSpec((nd, tm), lambda i, j: (0, i))],
        out_shape=[jax.ShapeDtypeStruct((t, n), bf16), jax.ShapeDtypeStruct((nd, t), f32)],
        scratch_shapes=scratch, compiler_params=params, name="in_proj_dt")(h, nw, w, wdt_t)


def _out_proj_kernel(y_ref, w_ref, h_ref, o_ref):
    o_ref[...] = h_ref[...] + jnp.dot(y_ref[...], w_ref[...], preferred_element_type=f32)


def _out_proj_final_kernel(y_ref, w_ref, h_ref, nw_ref, o_ref):
    acc = h_ref[...] + jnp.dot(y_ref[...], w_ref[...], preferred_element_type=f32)
    ms = jnp.mean(acc * acc, axis=-1, keepdims=True)
    o_ref[...] = acc * lax.rsqrt(ms + EPS) * nw_ref[...]


def _out_proj(y, w, h, final_nw=None, *, tm=1024):
    t, e = y.shape
    d = w.shape[1]
    in_specs = [pl.BlockSpec((tm, e), lambda i: (i, 0)),
                pl.BlockSpec((e, d), lambda i: (0, 0)),
                pl.BlockSpec((tm, d), lambda i: (i, 0))]
    args = [y, w, h]
    body = _out_proj_kernel
    if final_nw is not None:
        in_specs.append(pl.BlockSpec((1, d), lambda i: (0, 0)))
        args.append(final_nw)
        body = _out_proj_final_kernel
    return pl.pallas_call(
        body, grid=(t // tm,), in_specs=in_specs,
        out_specs=pl.BlockSpec((tm, d), lambda i: (i, 0)),
        out_shape=jax.ShapeDtypeStruct((t, d), f32),
        compiler_params=pltpu.CompilerParams(dimension_semantics=("arbitrary",),
                                             vmem_limit_bytes=VMEM_LIMIT),
        name="out_proj")(*args)


def _cumsum_lanes(v, lane):
    for s in (1, 2, 4, 8, 16, 32, 64):
        v = v + jnp.where(lane >= s, pltpu.roll(v, s, axis=1), 0.0)
    return v


def _expand_heads(col, k0, lane_head):
    c = [col[:, k0 + r:k0 + r + 1] for r in range(HEADS_PER_GROUP)]
    return jnp.where(lane_head == 0, c[0],
                     jnp.where(lane_head == 1, c[1], jnp.where(lane_head == 2, c[2], c[3])))


def _ssd_kernel(z_ref, x_ref, b_ref, c_ref, dt_ref, dtb_ref, alog_ref,
                cwx_ref, cwb_ref, cwc_ref, cbx_ref, cbb_ref, cbc_ref, dskip_ref, nw_ref,
                o_ref, xp_scr, xbc_scr, row_scr, col_scr, y_scr, h_scr):
    seq = x_ref.shape[0]
    n_chunks = seq // CHUNK
    width = GROUP_WIDTH + 2 * SSD_STATE

    xp_scr[0:HALO, :] = jnp.zeros((HALO, width), f32)
    xp_scr[seq + HALO:seq + 2 * HALO, :] = jnp.zeros((HALO, width), f32)

    def fill(c, carry):
        base = pl.multiple_of(c * CHUNK, CHUNK)
        rows = pl.ds(base + HALO, CHUNK)
        xp_scr[rows, 0:GROUP_WIDTH] = x_ref[pl.ds(base, CHUNK), :].astype(f32)
        xp_scr[rows, GROUP_WIDTH:GROUP_WIDTH + SSD_STATE] = b_ref[pl.ds(base, CHUNK), :].astype(f32)
        xp_scr[rows, GROUP_WIDTH + SSD_STATE:width] = c_ref[pl.ds(base, CHUNK), :].astype(f32)
        return carry

    lax.fori_loop(0, n_chunks, fill, 0)

    lane = lax.broadcasted_iota(jnp.int32, (8, CHUNK), 1)
    is_fwd = lax.broadcasted_iota(jnp.int32, (8, CHUNK), 0) < HEADS_PER_GROUP
    neg_a = jnp.exp(alog_ref[...])
    for c in range(n_chunks):
        raw = dt_ref[:, c * CHUNK:(c + 1) * CHUNK] + dtb_ref[...]
        dt = jnp.maximum(raw, 0.0) + jnp.log1p(jnp.exp(-jnp.abs(raw)))
        a = -(dt * neg_a)
        incl = _cumsum_lanes(a, lane)
        tot = incl[:, CHUNK - 1:CHUNK]
        cs = jnp.where(is_fwd, incl, incl - a)
        log_dt = jnp.log(dt)
        row_scr[c] = jnp.where(is_fwd, cs - log_dt, cs + log_dt)
        to_end = jnp.exp(tot - cs)
        from_start = jnp.exp(cs)
        state_w = jnp.where(is_fwd, to_end, from_start) * dt
        carry_w = jnp.where(is_fwd, from_start, to_end)
        stacked = jnp.concatenate([cs, state_w, carry_w, jnp.zeros((CHUNK - 24, CHUNK), f32)], axis=0)
        col_scr[c * CHUNK:(c + 1) * CHUNK, :] = stacked.T

    conv_w = jnp.concatenate([cwx_ref[...], cwb_ref[...], cwc_ref[...]], axis=1)
    conv_b = jnp.concatenate([cbx_ref[...], cbb_ref[...], cbc_ref[...]], axis=1)
    dskip = dskip_ref[...]
    nw = nw_ref[...]
    lane_head = lax.broadcasted_iota(jnp.int32, (CHUNK, GROUP_WIDTH), 1) // SSD_HEAD_DIM
    li = lax.broadcasted_iota(jnp.int32, (CHUNK, CHUNK), 0)
    si = lax.broadcasted_iota(jnp.int32, (CHUNK, CHUNK), 1)
    contract0 = (((0,), (0,)), ((), ()))
    contract1 = (((1,), (1,)), ((), ()))

    h_scr[...] = jnp.zeros_like(h_scr)

    def fwd(c, carry):
        base = pl.multiple_of(c * CHUNK, CHUNK)
        win = xp_scr[pl.ds(base, CHUNK + 2 * HALO), :]
        acc = jnp.broadcast_to(conv_b, (CHUNK, width))
        for k in range(SSD_CONV):
            acc = acc + _rows_ahead(win, HALO - SSD_CONV // 2 + k) * conv_w[k:k + 1, :]
        xbc = _silu(acc)
        xbc_scr[pl.ds(base, CHUNK), :] = xbc.astype(bf16)
        xs = xbc[:, 0:GROUP_WIDTH]
        bm = xbc[:, GROUP_WIDTH:GROUP_WIDTH + SSD_STATE].astype(bf16)
        cm = xbc[:, GROUP_WIDTH + SSD_STATE:width].astype(bf16)
        col = col_scr[pl.ds(base, CHUNK), :]
        carry_w = _expand_heads(col, 16, lane_head)
        state_w = _expand_heads(col, 8, lane_head)
        hf = h_scr[...]
        y_off = jnp.dot(cm, hf.astype(bf16), preferred_element_type=f32)
        y_scr[pl.ds(base, CHUNK), :] = carry_w * y_off + xs * dskip
        local = lax.dot_general(bm, (xs * state_w).astype(bf16), contract0,
                                preferred_element_type=f32)
        h_scr[...] = carry_w[CHUNK - 1:CHUNK, :] * hf + local
        return carry

    lax.fori_loop(0, n_chunks, fwd, 0)

    h_scr[...] = jnp.zeros_like(h_scr)
    lower = si <= li
    upper = si >= li

    def bwd(i, carry):
        c = n_chunks - 1 - i
        base = pl.multiple_of(c * CHUNK, CHUNK)
        xbc = xbc_scr[pl.ds(base, CHUNK), :]
        xs = xbc[:, 0:GROUP_WIDTH]
        bm = xbc[:, GROUP_WIDTH:GROUP_WIDTH + SSD_STATE]
        cm = xbc[:, GROUP_WIDTH + SSD_STATE:width]
        col = col_scr[pl.ds(base, CHUNK), :]
        row = row_scr[c]
        cb = lax.dot_general(cm, bm, contract1, preferred_element_type=f32)
        y = y_scr[pl.ds(base, CHUNK), :]
        for r in range(HEADS_PER_GROUP):
            arg_f = col[:, r:r + 1] - row[r:r + 1, :]
            arg_b = row[4 + r:5 + r, :] - col[:, 4 + r:5 + r]
            decay = (jnp.exp(jnp.where(lower, arg_f, -jnp.inf))
                     + jnp.exp(jnp.where(upper, arg_b, -jnp.inf)))
            g = (cb * decay).astype(bf16)
            x_head = jnp.where(lane_head == r, xs, jnp.zeros_like(xs))
            y = y + jnp.dot(g, x_head, preferred_element_type=f32)
        hb = h_scr[...]
        carry_w = _expand_heads(col, 20, lane_head)
        state_w = _expand_heads(col, 12, lane_head)
        y = y + carry_w * jnp.dot(cm, hb.astype(bf16), preferred_element_type=f32)
        y = y * _silu(z_ref[pl.ds(base, CHUNK), :].astype(f32))
        ms = jnp.mean(y * y, axis=-1, keepdims=True)
        o_ref[pl.ds(base, CHUNK), :] = (y * lax.rsqrt(ms + EPS) * nw).astype(o_ref.dtype)
        local = lax.dot_general(bm, (xs.astype(f32) * state_w).astype(bf16), contract0,
                                preferred_element_type=f32)
        h_scr[...] = carry_w[0:1, :] * hb + local
        return carry

    lax.fori_loop(0, n_chunks, bwd, 0)


def _ssd_core(proj, dt_t, dt_bias, a_log, conv_w, conv_b, dskip, norm_w):
    bsz, seq, _ = proj.shape
    e = SSD_GROUPS * GROUP_WIDTH
    width = GROUP_WIDTH + 2 * SSD_STATE
    n_x = e // GROUP_WIDTH
    xb = lambda blk, off: pl.BlockSpec((None, seq, blk), lambda b, g, off=off: (b, 0, off + g))
    in_specs = [
        xb(GROUP_WIDTH, 0),
        xb(GROUP_WIDTH, n_x),
        xb(SSD_STATE, 2 * e // SSD_STATE),
        xb(SSD_STATE, 2 * e // SSD_STATE + SSD_GROUPS),
        pl.BlockSpec((8, seq), lambda b, g: (g, b)),
        pl.BlockSpec((8, 1), lambda b, g: (g, 0)),
        pl.BlockSpec((8, 1), lambda b, g: (g, 0)),
        pl.BlockSpec((SSD_CONV, GROUP_WIDTH), lambda b, g: (0, g)),
        pl.BlockSpec((SSD_CONV, SSD_STATE), lambda b, g: (0, e // SSD_STATE + g)),
        pl.BlockSpec((SSD_CONV, SSD_STATE), lambda b, g: (0, e // SSD_STATE + SSD_GROUPS + g)),
        pl.BlockSpec((1, GROUP_WIDTH), lambda b, g: (0, g)),
        pl.BlockSpec((1, SSD_STATE), lambda b, g: (0, e // SSD_STATE + g)),
        pl.BlockSpec((1, SSD_STATE), lambda b, g: (0, e // SSD_STATE + SSD_GROUPS + g)),
        pl.BlockSpec((1, GROUP_WIDTH), lambda b, g: (0, g)),
        pl.BlockSpec((1, GROUP_WIDTH), lambda b, g: (0, g)),
    ]
    return pl.pallas_call(
        _ssd_kernel, grid=(bsz, SSD_GROUPS), in_specs=in_specs,
        out_specs=pl.BlockSpec((None, seq, GROUP_WIDTH), lambda b, g: (b, 0, g)),
        out_shape=jax.ShapeDtypeStruct((bsz, seq, e), bf16),
        scratch_shapes=[
            pltpu.VMEM((seq + 2 * HALO, width), f32),
            pltpu.VMEM((seq, width), bf16),
            pltpu.VMEM((seq // CHUNK, 8, CHUNK), f32),
            pltpu.VMEM((seq, CHUNK), f32),
            pltpu.VMEM((seq, GROUP_WIDTH), f32),
            pltpu.VMEM((SSD_STATE, GROUP_WIDTH), f32),
        ],
        compiler_params=pltpu.CompilerParams(dimension_semantics=("arbitrary", "arbitrary"),
                                             vmem_limit_bytes=VMEM_LIMIT),
        name="ssd_core")(proj, proj, proj, proj, dt_t, dt_bias, a_log,
                         conv_w, conv_w, conv_w, conv_b, conv_b, conv_b, dskip, norm_w)


def _window_sum(win, w):
    half = w // 2
    if w == 2:
        return _rows_ahead(win, HALO - 1) + _rows_ahead(win, HALO)
    t = win + pltpu.roll(win, win.shape[0] - 1, axis=0)
    span = 2
    while span * 2 < w:
        t = t + pltpu.roll(t, win.shape[0] - span, axis=0)
        span *= 2
    return _rows_ahead(t, HALO - half) + _rows_ahead(t, HALO)


def _pool_kernel(v_ref, gate_ref, mix_ref, scale_ref, o_ref, vp_scr):
    seq, gd = v_ref.shape
    n_steps = seq // POOL_ROWS
    gi = pl.program_id(1)

    vp_scr[0:HALO, :] = jnp.zeros((HALO, gd), f32)
    vp_scr[seq + HALO:seq + 2 * HALO, :] = jnp.zeros((HALO, gd), f32)

    def fill(c, carry):
        base = pl.multiple_of(c * POOL_ROWS, POOL_ROWS)
        vp_scr[pl.ds(base + HALO, POOL_ROWS), :] = v_ref[pl.ds(base, POOL_ROWS), :].astype(f32)
        return carry

    lax.fori_loop(0, n_steps, fill, 0)
    scale = scale_ref[...]

    for k, w in enumerate(POOL_WINDOWS):
        @pl.when(gi == k)
        def _(w=w):
            half = w // 2

            def step(c, carry):
                base = pl.multiple_of(c * POOL_ROWS, POOL_ROWS)
                win = vp_scr[pl.ds(base, POOL_ROWS + 2 * HALO), :]
                pos = base + lax.broadcasted_iota(jnp.int32, (POOL_ROWS, 1), 0)
                cnt = (jnp.minimum(pos + half, seq) - jnp.maximum(pos - half, 0)).astype(f32)
                pooled = _window_sum(win, w) / cnt - win[HALO:HALO + POOL_ROWS, :]
                mixed = jnp.dot(pooled.astype(bf16), mix_ref[...], preferred_element_type=f32)
                gate = gate_ref[pl.ds(base, POOL_ROWS), :].astype(f32)
                o_ref[pl.ds(base, POOL_ROWS), :] = (mixed * scale * _silu(gate)).astype(o_ref.dtype)
                return carry

            lax.fori_loop(0, n_steps, step, 0)


def _pool_core(proj, mix_w, scale):
    bsz, seq, e2 = proj.shape
    e = e2 // 2
    ng = len(POOL_WINDOWS)
    gd = e // ng
    return pl.pallas_call(
        _pool_kernel, grid=(bsz, ng),
        in_specs=[pl.BlockSpec((None, seq, gd), lambda b, g: (b, 0, g)),
                  pl.BlockSpec((None, seq, gd), lambda b, g: (b, 0, ng + g)),
                  pl.BlockSpec((None, gd, gd), lambda b, g: (g, 0, 0)),
                  pl.BlockSpec((1, gd), lambda b, g: (0, g))],
        out_specs=pl.BlockSpec((None, seq, gd), lambda b, g: (b, 0, g)),
        out_shape=jax.ShapeDtypeStruct((bsz, seq, e), bf16),
        scratch_shapes=[pltpu.VMEM((seq + 2 * HALO, gd), f32)],
        compiler_params=pltpu.CompilerParams(dimension_semantics=("arbitrary", "arbitrary"),
                                             vmem_limit_bytes=VMEM_LIMIT),
        name="pool_core")(proj, proj, mix_w, scale)


def _group_major(p):
    return p.reshape(2, SSD_GROUPS, HEADS_PER_GROUP).transpose(1, 0, 2).reshape(-1, 1)


def kernel(x, norm_w, ssd_w_in, ssd_conv_w, ssd_conv_b, ssd_dt_bias, ssd_a_log, ssd_d, ssd_norm_w, ssd_w_out, pool_w_in, pool_mix_w, pool_scale, pool_w_out, final_norm_w):
    bsz, seq, d = x.shape
    t = bsz * seq
    depth = norm_w.shape[0]
    e = ssd_w_out.shape[1]
    n_main = 2 * e + 2 * SSD_GROUPS * SSD_STATE
    h = x.reshape(t, d)
    for i in range(depth):
        j = i // 2
        nw = norm_w[i].reshape(1, d)
        final_nw = final_norm_w.reshape(1, d) if i == depth - 1 else None
        if i % 2 == 0:
            w_main = ssd_w_in[j][:, :n_main].astype(bf16)
            w_dt = ssd_w_in[j][:, n_main:]
            wdt_t = (w_dt.reshape(d, 2, SSD_GROUPS, HEADS_PER_GROUP).transpose(2, 1, 3, 0)
                     .reshape(-1, d).astype(bf16))
            proj, dt_t = _in_proj(h, nw, w_main, wdt_t)
            y = _ssd_core(proj.reshape(bsz, seq, n_main), dt_t,
                          _group_major(ssd_dt_bias[j]), _group_major(ssd_a_log[j]),
                          ssd_conv_w[j], ssd_conv_b[j].reshape(1, -1),
                          jnp.repeat(ssd_d[j], SSD_HEAD_DIM).reshape(1, e),
                          ssd_norm_w[j].reshape(1, e))
            h = _out_proj(y.reshape(t, e), ssd_w_out[j].astype(bf16), h, final_nw)
        else:
            proj = _in_proj(h, nw, pool_w_in[j].astype(bf16))
            y = _pool_core(proj.reshape(bsz, seq, 2 * e), pool_mix_w[j].astype(bf16),
                           pool_scale[j].reshape(1, e))
            h = _out_proj(y.reshape(t, e), pool_w_out[j].astype(bf16), h, final_nw)
    return h.reshape(bsz, seq, d)
```

```python
import functools

import jax
import jax.numpy as jnp
import numpy as np
from jax import lax
from jax.experimental import pallas as pl
from jax.experimental.pallas import tpu as pltpu

f32 = jnp.float32
bf16 = jnp.bfloat16

EPS = 1e-6
LOG2E = 1.4426950408889634
SSD_HEAD_DIM = 64
SSD_GROUPS = 8
SSD_STATE = 128
SSD_CONV = 5
HEADS_PER_GROUP = 4
GROUP_WIDTH = HEADS_PER_GROUP * SSD_HEAD_DIM
CHUNK = 128
HALO = 8
POOL_WINDOWS = (2, 4, 8, 16)
POOL_ROWS = 256

VMEM_LIMIT = 48 * 1024 * 1024


def _silu(v):
    half = 0.5 * v
    return half + half * jnp.tanh(half)


def _rows_ahead(win, k):
    n = win.shape[0]
    if k % 8 == 0:
        return win[k:k + n - 2 * HALO, :]
    return pltpu.roll(win, n - k, axis=0)[0:n - 2 * HALO, :]


def _norm_to_scratch(h_ref, nw_ref, u_scr):
    xv = h_ref[...]
    ms = jnp.mean(xv * xv, axis=-1, keepdims=True)
    u_scr[...] = (xv * lax.rsqrt(ms + EPS) * nw_ref[...]).astype(bf16)


def _in_proj_kernel(h_ref, nw_ref, w_ref, o_ref, u_scr):
    @pl.when(pl.program_id(1) == 0)
    def _():
        _norm_to_scratch(h_ref, nw_ref, u_scr)

    o_ref[...] = jnp.dot(u_scr[...], w_ref[...], preferred_element_type=f32).astype(o_ref.dtype)


def _in_proj_dt_kernel(h_ref, nw_ref, w_ref, wdt_ref, o_ref, dt_ref, u_scr):
    @pl.when(pl.program_id(1) == 0)
    def _():
        _norm_to_scratch(h_ref, nw_ref, u_scr)
        dt_ref[...] = lax.dot_general(wdt_ref[...], u_scr[...], (((1,), (1,)), ((), ())),
                                      preferred_element_type=f32)

    o_ref[...] = jnp.dot(u_scr[...], w_ref[...], preferred_element_type=f32).astype(o_ref.dtype)


def _in_proj(h, nw, w, wdt_t=None, *, tm=1024, tn=1024):
    t, d = h.shape
    n = w.shape[1]
    grid = (t // tm, n // tn)
    h_spec = pl.BlockSpec((tm, d), lambda i, j: (i, 0))
    nw_spec = pl.BlockSpec((1, d), lambda i, j: (0, 0))
    w_spec = pl.BlockSpec((d, tn), lambda i, j: (0, j))
    o_spec = pl.BlockSpec((tm, tn), lambda i, j: (i, j))
    params = pltpu.CompilerParams(dimension_semantics=("arbitrary", "arbitrary"),
                                  vmem_limit_bytes=VMEM_LIMIT)
    scratch = [pltpu.VMEM((tm, d), bf16)]
    if wdt_t is None:
        return pl.pallas_call(
            _in_proj_kernel, grid=grid, in_specs=[h_spec, nw_spec, w_spec], out_specs=o_spec,
            out_shape=jax.ShapeDtypeStruct((t, n), bf16), scratch_shapes=scratch,
            compiler_params=params, name="in_proj")(h, nw, w)
    nd = wdt_t.shape[0]
    return pl.pallas_call(
        _in_proj_dt_kernel, grid=grid,
        in_specs=[h_spec, nw_spec, w_spec, pl.BlockSpec((nd, d), lambda i, j: (0, 0))],
        out_specs=[o_spec, pl.BlockSpec((nd, tm), lambda i, j: (0, i))],
        out_shape=[jax.ShapeDtypeStruct((t, n), bf16), jax.ShapeDtypeStruct((nd, t), f32)],
        scratch_shapes=scratch, compiler_params=params, name="in_proj_dt")(h, nw, w, wdt_t)


def _out_proj_kernel(y_ref, w_ref, h_ref, o_ref):
    o_ref[...] = h_ref[...] + jnp.dot(y_ref[...], w_ref[...], preferred_element_type=f32)


def _out_proj_final_kernel(y_ref, w_ref, h_ref, nw_ref, o_ref):
    acc = h_ref[...] + jnp.dot(y_ref[...], w_ref[...], preferred_element_type=f32)
    ms = jnp.mean(acc * acc, axis=-1, keepdims=True)
    o_ref[...] = acc * lax.rsqrt(ms + EPS) * nw_ref[...]


def _out_proj(y, w, h, final_nw=None, *, tm=1024):
    t, e = y.shape
    d = w.shape[1]
    in_specs = [pl.BlockSpec((tm, e), lambda i: (i, 0)),
                pl.BlockSpec((e, d), lambda i: (0, 0)),
                pl.BlockSpec((tm, d), lambda i: (i, 0))]
    args = [y, w, h]
    body = _out_proj_kernel
    if final_nw is not None:
        in_specs.append(pl.BlockSpec((1, d), lambda i: (0, 0)))
        args.append(final_nw)
        body = _out_proj_final_kernel
    return pl.pallas_call(
        body, grid=(t // tm,), in_specs=in_specs,
        out_specs=pl.BlockSpec((tm, d), lambda i: (i, 0)),
        out_shape=jax.ShapeDtypeStruct((t, d), f32),
        compiler_params=pltpu.CompilerParams(dimension_semantics=("arbitrary",),
                                             vmem_limit_bytes=VMEM_LIMIT),
        name="out_proj")(*args)


def _cumsum_lanes(v, lane):
    for s in (1, 2, 4, 8, 16, 32, 64):
        v = v + jnp.where(lane >= s, pltpu.roll(v, s, axis=1), 0.0)
    return v


def _split3(v):
    hi = v.astype(bf16).astype(f32)
    rem = v - hi
    mid = rem.astype(bf16).astype(f32)
    return hi, mid, rem - mid


_N_VALS = 24
_SPLIT_ROWS = 3 * _N_VALS
_STRIP_BASE = 80
_STRIP_ROWS = 48


def _selectors():
    n_dir = 2 * HEADS_PER_GROUP
    sel_g = np.zeros((_STRIP_BASE, n_dir * CHUNK), np.float32)
    for j in range(n_dir):
        for p in range(3):
            sel_g[p * _N_VALS + j, j * CHUNK:(j + 1) * CHUNK] = 1.0 if j < HEADS_PER_GROUP else -1.0
    sels = []
    for fwd_v, bwd_v in ((8, 12), (16, 20)):
        s = np.zeros((CHUNK, 2 * GROUP_WIDTH), np.float32)
        for a, v0 in enumerate((fwd_v, bwd_v)):
            for r in range(HEADS_PER_GROUP):
                lo = a * GROUP_WIDTH + r * SSD_HEAD_DIM
                for p in range(3):
                    s[p * _N_VALS + v0 + r, lo:lo + SSD_HEAD_DIM] = 1.0
        sels.append(s)
    return (jnp.asarray(sel_g, bf16), jnp.asarray(sels[0], bf16), jnp.asarray(sels[1], bf16))


def _ssd_kernel(z_ref, x_ref, b_ref, c_ref, dt_ref, dtb_ref, alog_ref,
                cwx_ref, cwb_ref, cwc_ref, cbx_ref, cbb_ref, cbc_ref, dskip_ref, nw_ref,
                selg_ref, sels_ref, selc_ref,
                o_ref, xp_scr, xbc_scr, strip_scr, col_scr, y_scr, s_scr, edge_scr, hin_scr,
                exp_scr, cb_scr):
    seq = x_ref.shape[0]
    n_chunks = seq // CHUNK
    width = GROUP_WIDTH + 2 * SSD_STATE

    xp_scr[0:HALO, :] = jnp.zeros((HALO, width), f32)
    xp_scr[seq + HALO:seq + 2 * HALO, :] = jnp.zeros((HALO, width), f32)

    def fill(c, carry):
        base = pl.multiple_of(c * CHUNK, CHUNK)
        rows = pl.ds(base + HALO, CHUNK)
        xp_scr[rows, 0:GROUP_WIDTH] = x_ref[pl.ds(base, CHUNK), :].astype(f32)
        xp_scr[rows, GROUP_WIDTH:GROUP_WIDTH + SSD_STATE] = b_ref[pl.ds(base, CHUNK), :].astype(f32)
        xp_scr[rows, GROUP_WIDTH + SSD_STATE:width] = c_ref[pl.ds(base, CHUNK), :].astype(f32)
        return carry

    lax.fori_loop(0, n_chunks, fill, 0)

    lane = lax.broadcasted_iota(jnp.int32, (8, CHUNK), 1)
    is_fwd = lax.broadcasted_iota(jnp.int32, (8, CHUNK), 0) < HEADS_PER_GROUP
    neg_a2 = jnp.exp(alog_ref[...]) * LOG2E
    n_dir = 2 * HEADS_PER_GROUP
    own_block = (lax.broadcasted_iota(jnp.int32, (8, n_dir * CHUNK), 1) // CHUNK
                 == lax.broadcasted_iota(jnp.int32, (8, n_dir * CHUNK), 0))
    eye = (lax.broadcasted_iota(jnp.int32, (CHUNK, CHUNK), 0)
           == lax.broadcasted_iota(jnp.int32, (CHUNK, CHUNK), 1)).astype(bf16)
    ones_rows = jnp.ones((CHUNK - _SPLIT_ROWS, CHUNK), f32)
    zero_rows = jnp.zeros((8, n_dir * CHUNK), f32)
    contract1 = (((1,), (1,)), ((), ()))
    for c in range(n_chunks):
        raw = dt_ref[:, c * CHUNK:(c + 1) * CHUNK] + dtb_ref[...]
        dt = jnp.maximum(raw, 0.0) + jnp.log1p(jnp.exp(-jnp.abs(raw)))
        a2 = -(dt * neg_a2)
        incl = _cumsum_lanes(a2, lane)
        tot = incl[:, CHUNK - 1:CHUNK]
        cs = jnp.where(is_fwd, incl, incl - a2)
        log2_dt = jnp.maximum(jnp.log2(dt), -1e30)
        src = jnp.where(is_fwd, log2_dt - cs, cs + log2_dt)
        to_end = jnp.exp2(tot - cs)
        from_start = jnp.exp2(cs)
        state_w = jnp.where(is_fwd, to_end, from_start) * dt
        carry_w = jnp.where(is_fwd, from_start, to_end)
        parts = _split3(jnp.concatenate([cs, state_w, carry_w], axis=0))
        rows = jnp.concatenate(list(parts) + [ones_rows], axis=0).astype(bf16)
        col_scr[c * CHUNK:(c + 1) * CHUNK, :] = lax.dot_general(
            eye, rows, contract1, preferred_element_type=f32).astype(bf16)
        strip = []
        for part in _split3(src):
            tiled = jnp.concatenate([part] * n_dir, axis=1)
            strip += [jnp.where(own_block, tiled, 0.0), zero_rows]
        strip_scr[c] = jnp.concatenate(strip, axis=0).astype(bf16)

    conv_w = jnp.concatenate([cwx_ref[...], cwb_ref[...], cwc_ref[...]], axis=1)
    conv_b = jnp.concatenate([cbx_ref[...], cbb_ref[...], cbc_ref[...]], axis=1)
    dskip = dskip_ref[...]
    nw = nw_ref[...]
    lane_head = lax.broadcasted_iota(jnp.int32, (CHUNK, GROUP_WIDTH), 1) // SSD_HEAD_DIM
    li = lax.broadcasted_iota(jnp.int32, (CHUNK, CHUNK), 0)
    si = lax.broadcasted_iota(jnp.int32, (CHUNK, CHUNK), 1)
    contract0 = (((0,), (0,)), ((), ()))

    def local_pass(c, carry):
        base = pl.multiple_of(c * CHUNK, CHUNK)
        win = xp_scr[pl.ds(base, CHUNK + 2 * HALO), :]
        acc = jnp.broadcast_to(conv_b, (CHUNK, width))
        for k in range(SSD_CONV):
            acc = acc + _rows_ahead(win, HALO - SSD_CONV // 2 + k) * conv_w[k:k + 1, :]
        xbc = _silu(acc)
        xbc_scr[pl.ds(base, CHUNK), :] = xbc.astype(bf16)
        xs = xbc[:, 0:GROUP_WIDTH]
        bm = xbc[:, GROUP_WIDTH:GROUP_WIDTH + SSD_STATE].astype(bf16)
        cm = xbc[:, GROUP_WIDTH + SSD_STATE:width].astype(bf16)
        y_scr[pl.ds(base, CHUNK), :] = xs * dskip
        col = col_scr[pl.ds(base, CHUNK), :]
        exp_scr[c] = jnp.dot(col, jnp.concatenate([selg_ref[...], strip_scr[c]], axis=0),
                             preferred_element_type=f32)
        cb_scr[c] = lax.dot_general(cm, bm, contract1, preferred_element_type=f32)
        state_w = jnp.dot(col, sels_ref[...], preferred_element_type=f32)
        weighted = jnp.concatenate([(xs * state_w[:, 0:GROUP_WIDTH]).astype(bf16),
                                    (xs * state_w[:, GROUP_WIDTH:]).astype(bf16)], axis=1)
        s_scr[c] = lax.dot_general(bm, weighted, contract0, preferred_element_type=f32)
        edge = jnp.concatenate([col[CHUNK - 16:CHUNK, :], col[0:16, :]], axis=0)
        edge_scr[c] = jnp.dot(edge, selc_ref[...], preferred_element_type=f32)
        return carry

    lax.fori_loop(0, n_chunks, local_pass, 0, unroll=8)

    def recur(i, carry):
        hf, hb = carry
        cf = i
        cr = n_chunks - 1 - i
        hin_scr[cf, :, 0:GROUP_WIDTH] = hf.astype(bf16)
        hin_scr[cr, :, GROUP_WIDTH:2 * GROUP_WIDTH] = hb.astype(bf16)
        hf = edge_scr[cf][15:16, 0:GROUP_WIDTH] * hf + s_scr[cf][:, 0:GROUP_WIDTH]
        hb = edge_scr[cr][16:17, GROUP_WIDTH:2 * GROUP_WIDTH] * hb + s_scr[cr][:, GROUP_WIDTH:2 * GROUP_WIDTH]
        return hf, hb

    h0 = jnp.zeros((SSD_STATE, GROUP_WIDTH), f32)
    lax.fori_loop(0, n_chunks, recur, (h0, h0))

    lower = si <= li
    upper = si >= li

    def output_pass(c, carry):
        base = pl.multiple_of(c * CHUNK, CHUNK)
        xbc = xbc_scr[pl.ds(base, CHUNK), :]
        xs = xbc[:, 0:GROUP_WIDTH]
        cm = xbc[:, GROUP_WIDTH + SSD_STATE:width]
        col = col_scr[pl.ds(base, CHUNK), :]
        carried = (jnp.dot(col, selc_ref[...], preferred_element_type=f32)
                   * jnp.dot(cm, hin_scr[c], preferred_element_type=f32))
        cb = cb_scr[c]
        g_heads, x_heads = [], []
        for r in range(HEADS_PER_GROUP):
            arg_f = exp_scr[c, :, r * CHUNK:(r + 1) * CHUNK]
            arg_b = exp_scr[c, :, (HEADS_PER_GROUP + r) * CHUNK:(HEADS_PER_GROUP + r + 1) * CHUNK]
            decay = (jnp.exp2(jnp.where(lower, arg_f, -jnp.inf))
                     + jnp.exp2(jnp.where(upper, arg_b, -jnp.inf)))
            g_heads.append((cb * decay).astype(bf16))
            x_heads.append(jnp.where(lane_head == r, xs, jnp.zeros_like(xs)))
        y = y_scr[pl.ds(base, CHUNK), :] + jnp.dot(
            jnp.concatenate(g_heads, axis=1), jnp.concatenate(x_heads, axis=0),
            preferred_element_type=f32)
        y = y + carried[:, 0:GROUP_WIDTH] + carried[:, GROUP_WIDTH:]
        y = y * _silu(z_ref[pl.ds(base, CHUNK), :].astype(f32))
        ms = jnp.mean(y * y, axis=-1, keepdims=True)
        o_ref[pl.ds(base, CHUNK), :] = (y * lax.rsqrt(ms + EPS) * nw).astype(o_ref.dtype)
        return carry

    lax.fori_loop(0, n_chunks, output_pass, 0, unroll=8)


def _ssd_core(proj, dt_t, dt_bias, a_log, conv_w, conv_b, dskip, norm_w):
    bsz, seq, _ = proj.shape
    e = SSD_GROUPS * GROUP_WIDTH
    width = GROUP_WIDTH + 2 * SSD_STATE
    n_x = e // GROUP_WIDTH
    xb = lambda blk, off: pl.BlockSpec((None, seq, blk), lambda b, g, off=off: (b, 0, off + g))
    in_specs = [
        xb(GROUP_WIDTH, 0),
        xb(GROUP_WIDTH, n_x),
        xb(SSD_STATE, 2 * e // SSD_STATE),
        xb(SSD_STATE, 2 * e // SSD_STATE + SSD_GROUPS),
        pl.BlockSpec((8, seq), lambda b, g: (g, b)),
        pl.BlockSpec((8, 1), lambda b, g: (g, 0)),
        pl.BlockSpec((8, 1), lambda b, g: (g, 0)),
        pl.BlockSpec((SSD_CONV, GROUP_WIDTH), lambda b, g: (0, g)),
        pl.BlockSpec((SSD_CONV, SSD_STATE), lambda b, g: (0, e // SSD_STATE + g)),
        pl.BlockSpec((SSD_CONV, SSD_STATE), lambda b, g: (0, e // SSD_STATE + SSD_GROUPS + g)),
        pl.BlockSpec((1, GROUP_WIDTH), lambda b, g: (0, g)),
        pl.BlockSpec((1, SSD_STATE), lambda b, g: (0, e // SSD_STATE + g)),
        pl.BlockSpec((1, SSD_STATE), lambda b, g: (0, e // SSD_STATE + SSD_GROUPS + g)),
        pl.BlockSpec((1, GROUP_WIDTH), lambda b, g: (0, g)),
        pl.BlockSpec((1, GROUP_WIDTH), lambda b, g: (0, g)),
    ]
    selectors = _selectors()
    in_specs += [pl.BlockSpec(s.shape, lambda b, g: (0, 0)) for s in selectors]
    return pl.pallas_call(
        _ssd_kernel, grid=(bsz, SSD_GROUPS), in_specs=in_specs,
        out_specs=pl.BlockSpec((None, seq, GROUP_WIDTH), lambda b, g: (b, 0, g)),
        out_shape=jax.ShapeDtypeStruct((bsz, seq, e), bf16),
        scratch_shapes=[
            pltpu.VMEM((seq + 2 * HALO, width), f32),
            pltpu.VMEM((seq, width), bf16),
            pltpu.VMEM((seq // CHUNK, _STRIP_ROWS, 2 * HEADS_PER_GROUP * CHUNK), bf16),
            pltpu.VMEM((seq, CHUNK), bf16),
            pltpu.VMEM((seq, GROUP_WIDTH), f32),
            pltpu.VMEM((seq // CHUNK, SSD_STATE, 2 * GROUP_WIDTH), f32),
            pltpu.VMEM((seq // CHUNK, 32, 2 * GROUP_WIDTH), f32),
            pltpu.VMEM((seq // CHUNK, SSD_STATE, 2 * GROUP_WIDTH), bf16),
            pltpu.VMEM((seq // CHUNK, CHUNK, 2 * HEADS_PER_GROUP * CHUNK), f32),
            pltpu.VMEM((seq // CHUNK, CHUNK, CHUNK), f32),
        ],
        compiler_params=pltpu.CompilerParams(dimension_semantics=("arbitrary", "arbitrary"),
                                             vmem_limit_bytes=VMEM_LIMIT),
        name="ssd_core")(proj, proj, proj, proj, dt_t, dt_bias, a_log,
                         conv_w, conv_w, conv_w, conv_b, conv_b, conv_b, dskip, norm_w,
                         *selectors)


def _window_sum(win, w):
    half = w // 2
    if w == 2:
        return _rows_ahead(win, HALO - 1) + _rows_ahead(win, HALO)
    t = win + pltpu.roll(win, win.shape[0] - 1, axis=0)
    span = 2
    while span * 2 < w:
        t = t + pltpu.roll(t, win.shape[0] - span, axis=0)
        span *= 2
    return _rows_ahead(t, HALO - half) + _rows_ahead(t, HALO)


def _pool_kernel(v_ref, gate_ref, mix_ref, scale_ref, o_ref, vp_scr):
    seq, gd = v_ref.shape
    n_steps = seq // POOL_ROWS
    gi = pl.program_id(1)

    vp_scr[0:HALO, :] = jnp.zeros((HALO, gd), f32)
    vp_scr[seq + HALO:seq + 2 * HALO, :] = jnp.zeros((HALO, gd), f32)

    def fill(c, carry):
        base = pl.multiple_of(c * POOL_ROWS, POOL_ROWS)
        vp_scr[pl.ds(base + HALO, POOL_ROWS), :] = v_ref[pl.ds(base, POOL_ROWS), :].astype(f32)
        return carry

    lax.fori_loop(0, n_steps, fill, 0)
    scale = scale_ref[...]

    for k, w in enumerate(POOL_WINDOWS):
        @pl.when(gi == k)
        def _(w=w):
            half = w // 2

            def step(c, carry):
                base = pl.multiple_of(c * POOL_ROWS, POOL_ROWS)
                win = vp_scr[pl.ds(base, POOL_ROWS + 2 * HALO), :]
                pos = base + lax.broadcasted_iota(jnp.int32, (POOL_ROWS, 1), 0)
                cnt = (jnp.minimum(pos + half, seq) - jnp.maximum(pos - half, 0)).astype(f32)
                pooled = _window_sum(win, w) / cnt - win[HALO:HALO + POOL_ROWS, :]
                mixed = jnp.dot(pooled.astype(bf16), mix_ref[...], preferred_element_type=f32)
                gate = gate_ref[pl.ds(base, POOL_ROWS), :].astype(f32)
                o_ref[pl.ds(base, POOL_ROWS), :] = (mixed * scale * _silu(gate)).astype(o_ref.dtype)
                return carry

            lax.fori_loop(0, n_steps, step, 0)


def _pool_core(proj, mix_w, scale):
    bsz, seq, e2 = proj.shape
    e = e2 // 2
    ng = len(POOL_WINDOWS)
    gd = e // ng
    return pl.pallas_call(
        _pool_kernel, grid=(bsz, ng),
        in_specs=[pl.BlockSpec((None, seq, gd), lambda b, g: (b, 0, g)),
                  pl.BlockSpec((None, seq, gd), lambda b, g: (b, 0, ng + g)),
                  pl.BlockSpec((None, gd, gd), lambda b, g: (g, 0, 0)),
                  pl.BlockSpec((1, gd), lambda b, g: (0, g))],
        out_specs=pl.BlockSpec((None, seq, gd), lambda b, g: (b, 0, g)),
        out_shape=jax.ShapeDtypeStruct((bsz, seq, e), bf16),
        scratch_shapes=[pltpu.VMEM((seq + 2 * HALO, gd), f32)],
        compiler_params=pltpu.CompilerParams(dimension_semantics=("arbitrary", "arbitrary"),
                                             vmem_limit_bytes=VMEM_LIMIT),
        name="pool_core")(proj, proj, mix_w, scale)


def _group_major(p):
    return p.reshape(2, SSD_GROUPS, HEADS_PER_GROUP).transpose(1, 0, 2).reshape(-1, 1)


def kernel(x, norm_w, ssd_w_in, ssd_conv_w, ssd_conv_b, ssd_dt_bias, ssd_a_log, ssd_d, ssd_norm_w, ssd_w_out, pool_w_in, pool_mix_w, pool_scale, pool_w_out, final_norm_w):
    bsz, seq, d = x.shape
    t = bsz * seq
    depth = norm_w.shape[0]
    e = ssd_w_out.shape[1]
    n_main = 2 * e + 2 * SSD_GROUPS * SSD_STATE
    h = x.reshape(t, d)
    for i in range(depth):
        j = i // 2
        nw = norm_w[i].reshape(1, d)
        final_nw = final_norm_w.reshape(1, d) if i == depth - 1 else None
        if i % 2 == 0:
            w_main = ssd_w_in[j][:, :n_main].astype(bf16)
            w_dt = ssd_w_in[j][:, n_main:]
            wdt_t = (w_dt.reshape(d, 2, SSD_GROUPS, HEADS_PER_GROUP).transpose(2, 1, 3, 0)
                     .reshape(-1, d).astype(bf16))
            proj, dt_t = _in_proj(h, nw, w_main, wdt_t)
            y = _ssd_core(proj.reshape(bsz, seq, n_main), dt_t,
                          _group_major(ssd_dt_bias[j]), _group_major(ssd_a_log[j]),
                          ssd_conv_w[j], ssd_conv_b[j].reshape(1, -1),
                          jnp.repeat(ssd_d[j], SSD_HEAD_DIM).reshape(1, e),
                          ssd_norm_w[j].reshape(1, e))
            h = _out_proj(y.reshape(t, e), ssd_w_out[j].astype(bf16), h, final_nw)
        else:
            proj = _in_proj(h, nw, pool_w_in[j].astype(bf16))
            y = _pool_core(proj.reshape(bsz, seq, 2 * e), pool_mix_w[j].astype(bf16),
                           pool_scale[j].reshape(1, e))
            h = _out_proj(y.reshape(t, e), pool_w_out[j].astype(bf16), h, final_nw)
    return h.reshape(bsz, seq, d)
```

```python
import functools

import jax
import jax.numpy as jnp
import numpy as np
from jax import lax
from jax.experimental import pallas as pl
from jax.experimental.pallas import tpu as pltpu

f32 = jnp.float32
bf16 = jnp.bfloat16

EPS = 1e-6
LOG2E = 1.4426950408889634
SSD_HEAD_DIM = 64
SSD_GROUPS = 8
SSD_STATE = 128
SSD_CONV = 5
HEADS_PER_GROUP = 4
GROUP_WIDTH = HEADS_PER_GROUP * SSD_HEAD_DIM
CHUNK = 128
HALO = 8
POOL_WINDOWS = (2, 4, 8, 16)
POOL_ROWS = 256

VMEM_LIMIT = 48 * 1024 * 1024


def _silu_of_twice(half):
    return half + half * jnp.tanh(half)


def _rows_ahead(win, k):
    n = win.shape[0]
    if k % 8 == 0:
        return win[k:k + n - 2 * HALO, :]
    return pltpu.roll(win, n - k, axis=0)[0:n - 2 * HALO, :]


def _norm_to_scratch(h_ref, nw_ref, u_scr):
    xv = h_ref[...]
    ms = jnp.mean(xv * xv, axis=-1, keepdims=True)
    u_scr[...] = (xv * lax.rsqrt(ms + EPS) * nw_ref[...]).astype(bf16)


def _in_proj_kernel(h_ref, nw_ref, w_ref, o_ref, u_scr):
    @pl.when(pl.program_id(1) == 0)
    def _():
        _norm_to_scratch(h_ref, nw_ref, u_scr)

    o_ref[...] = jnp.dot(u_scr[...], w_ref[...], preferred_element_type=f32).astype(o_ref.dtype)


def _in_proj_dt_kernel(h_ref, nw_ref, w_ref, wdt_ref, o_ref, dt_ref, u_scr):
    @pl.when(pl.program_id(1) == 0)
    def _():
        _norm_to_scratch(h_ref, nw_ref, u_scr)
        dt_ref[...] = lax.dot_general(wdt_ref[...], u_scr[...], (((1,), (1,)), ((), ())),
                                      preferred_element_type=f32)

    o_ref[...] = jnp.dot(u_scr[...], w_ref[...], preferred_element_type=f32).astype(o_ref.dtype)


def _in_proj(h, nw, w, wdt_t=None, *, tm=1024, tn=1024):
    t, d = h.shape
    n = w.shape[1]
    grid = (t // tm, n // tn)
    h_spec = pl.BlockSpec((tm, d), lambda i, j: (i, 0))
    nw_spec = pl.BlockSpec((1, d), lambda i, j: (0, 0))
    w_spec = pl.BlockSpec((d, tn), lambda i, j: (0, j))
    o_spec = pl.BlockSpec((tm, tn), lambda i, j: (i, j))
    params = pltpu.CompilerParams(dimension_semantics=("arbitrary", "arbitrary"),
                                  vmem_limit_bytes=VMEM_LIMIT)
    scratch = [pltpu.VMEM((tm, d), bf16)]
    if wdt_t is None:
        return pl.pallas_call(
            _in_proj_kernel, grid=grid, in_specs=[h_spec, nw_spec, w_spec], out_specs=o_spec,
            out_shape=jax.ShapeDtypeStruct((t, n), bf16), scratch_shapes=scratch,
            compiler_params=params, name="in_proj")(h, nw, w)
    nd = wdt_t.shape[0]
    return pl.pallas_call(
        _in_proj_dt_kernel, grid=grid,
        in_specs=[h_spec, nw_spec, w_spec, pl.BlockSpec((nd, d), lambda i, j: (0, 0))],
        out_specs=[o_spec, pl.BlockSpec((nd, tm), lambda i, j: (0, i))],
        out_shape=[jax.ShapeDtypeStruct((t, n), bf16), jax.ShapeDtypeStruct((nd, t), f32)],
        scratch_shapes=scratch, compiler_params=params, name="in_proj_dt")(h, nw, w, wdt_t)


def _out_proj_kernel(y_ref, w_ref, h_ref, o_ref):
    o_ref[...] = h_ref[...] + jnp.dot(y_ref[...], w_ref[...], preferred_element_type=f32)


def _out_proj_final_kernel(y_ref, w_ref, h_ref, nw_ref, o_ref):
    acc = h_ref[...] + jnp.dot(y_ref[...], w_ref[...], preferred_element_type=f32)
    ms = jnp.mean(acc * acc, axis=-1, keepdims=True)
    o_ref[...] = acc * lax.rsqrt(ms + EPS) * nw_ref[...]


def _out_proj(y, w, h, final_nw=None, *, tm=1024):
    t, e = y.shape
    d = w.shape[1]
    in_specs = [pl.BlockSpec((tm, e), lambda i: (i, 0)),
                pl.BlockSpec((e, d), lambda i: (0, 0)),
                pl.BlockSpec((tm, d), lambda i: (i, 0))]
    args = [y, w, h]
    body = _out_proj_kernel
    if final_nw is not None:
        in_specs.append(pl.BlockSpec((1, d), lambda i: (0, 0)))
        args.append(final_nw)
        body = _out_proj_final_kernel
    return pl.pallas_call(
        body, grid=(t // tm,), in_specs=in_specs,
        out_specs=pl.BlockSpec((tm, d), lambda i: (i, 0)),
        out_shape=jax.ShapeDtypeStruct((t, d), f32),
        compiler_params=pltpu.CompilerParams(dimension_semantics=("arbitrary",),
                                             vmem_limit_bytes=VMEM_LIMIT),
        name="out_proj")(*args)


def _split3(v):
    hi = v.astype(bf16).astype(f32)
    rem = v - hi
    mid = rem.astype(bf16).astype(f32)
    return hi, mid, rem - mid


_N_VALS = 24
_SPLIT_ROWS = 3 * _N_VALS
_STRIP_BASE = 80
_STRIP_ROWS = 48
_MASKED = -1e30


def _selectors():
    n_dir = 2 * HEADS_PER_GROUP
    sel_g = np.zeros((_STRIP_BASE, n_dir * CHUNK), np.float32)
    for j in range(n_dir):
        for p in range(3):
            sel_g[p * _N_VALS + j, j * CHUNK:(j + 1) * CHUNK] = 1.0 if j < HEADS_PER_GROUP else -1.0
    sels = []
    for fwd_v, bwd_v in ((8, 12), (16, 20)):
        s = np.zeros((CHUNK, 2 * GROUP_WIDTH), np.float32)
        for a, v0 in enumerate((fwd_v, bwd_v)):
            for r in range(HEADS_PER_GROUP):
                lo = a * GROUP_WIDTH + r * SSD_HEAD_DIM
                for p in range(3):
                    s[p * _N_VALS + v0 + r, lo:lo + SSD_HEAD_DIM] = 1.0
        sels.append(s)
    li = np.arange(CHUNK)[:, None]
    si = np.arange(CHUNK)[None, :]
    mask = np.concatenate([np.where(si > li, _MASKED, 0.0)] * HEADS_PER_GROUP
                          + [np.where(si < li, _MASKED, 0.0)] * HEADS_PER_GROUP, axis=1)
    return (jnp.asarray(sel_g, bf16), jnp.asarray(mask, bf16),
            jnp.asarray(sels[0], bf16), jnp.asarray(sels[1], bf16))


def _ssd_kernel(z_ref, x_ref, b_ref, c_ref, dt_ref, dtb_ref, alog_ref,
                cwx_ref, cwb_ref, cwc_ref, cbx_ref, cbb_ref, cbc_ref, dskip_ref, nw_ref,
                selg_ref, mask_ref, sels_ref, selc_ref,
                o_ref, xp_scr, xbc_scr, strip_scr, col_scr, y_scr, s_scr, edge_scr, hin_scr,
                exp_scr, cb_scr):
    seq = x_ref.shape[0]
    n_chunks = seq // CHUNK
    width = GROUP_WIDTH + 2 * SSD_STATE

    xp_scr[0:HALO, :] = jnp.zeros((HALO, width), f32)
    xp_scr[seq + HALO:seq + 2 * HALO, :] = jnp.zeros((HALO, width), f32)

    def fill(c, carry):
        base = pl.multiple_of(c * CHUNK, CHUNK)
        rows = pl.ds(base + HALO, CHUNK)
        xp_scr[rows, 0:GROUP_WIDTH] = x_ref[pl.ds(base, CHUNK), :].astype(f32)
        xp_scr[rows, GROUP_WIDTH:GROUP_WIDTH + SSD_STATE] = b_ref[pl.ds(base, CHUNK), :].astype(f32)
        xp_scr[rows, GROUP_WIDTH + SSD_STATE:width] = c_ref[pl.ds(base, CHUNK), :].astype(f32)
        return carry

    lax.fori_loop(0, n_chunks, fill, 0)

    n_dir = 2 * HEADS_PER_GROUP
    stack = lambda v: jnp.concatenate([v] * n_chunks, axis=0)
    raw = (jnp.concatenate([dt_ref[:, c * CHUNK:(c + 1) * CHUNK] for c in range(n_chunks)], axis=0)
           + stack(dtb_ref[...]))
    is_fwd = (lax.broadcasted_iota(jnp.int32, raw.shape, 0) & (n_dir - 1)) < HEADS_PER_GROUP
    dt = jnp.maximum(raw, 0.0) + jnp.log1p(jnp.exp(-jnp.abs(raw)))
    a2 = -(dt * stack(jnp.exp(alog_ref[...]) * LOG2E))
    tri = (lax.broadcasted_iota(jnp.int32, (CHUNK, CHUNK), 0)
           <= lax.broadcasted_iota(jnp.int32, (CHUNK, CHUNK), 1)).astype(bf16)
    sums = jnp.dot(jnp.concatenate(_split3(a2), axis=0).astype(bf16), tri, preferred_element_type=f32)
    n_rows = raw.shape[0]
    incl = sums[0:n_rows] + sums[n_rows:2 * n_rows] + sums[2 * n_rows:3 * n_rows]
    tot = incl[:, CHUNK - 1:CHUNK]
    cs = jnp.where(is_fwd, incl, incl - a2)
    log2_dt = jnp.maximum(jnp.log2(dt), -1e30)
    src_parts = _split3(jnp.where(is_fwd, log2_dt - cs, cs + log2_dt))
    to_end = jnp.exp2(tot - cs)
    from_start = jnp.exp2(cs)
    state_w = jnp.where(is_fwd, to_end, from_start) * dt
    carry_w = jnp.where(is_fwd, from_start, to_end)
    val_parts = list(zip(_split3(cs), _split3(state_w), _split3(carry_w)))

    own_block = (lax.broadcasted_iota(jnp.int32, (8, n_dir * CHUNK), 1) // CHUNK
                 == lax.broadcasted_iota(jnp.int32, (8, n_dir * CHUNK), 0))
    eye = (lax.broadcasted_iota(jnp.int32, (CHUNK, CHUNK), 0)
           == lax.broadcasted_iota(jnp.int32, (CHUNK, CHUNK), 1)).astype(bf16)
    ones_rows = jnp.ones((CHUNK - _SPLIT_ROWS, CHUNK), f32)
    zero_rows = jnp.zeros((8, n_dir * CHUNK), f32)
    contract1 = (((1,), (1,)), ((), ()))
    for c in range(n_chunks):
        own = slice(c * n_dir, (c + 1) * n_dir)
        rows = jnp.concatenate([v[own] for part in val_parts for v in part] + [ones_rows],
                               axis=0).astype(bf16)
        col_scr[c * CHUNK:(c + 1) * CHUNK, :] = lax.dot_general(
            eye, rows, contract1, preferred_element_type=f32).astype(bf16)
        strip = []
        for part in src_parts:
            tiled = jnp.concatenate([part[own]] * n_dir, axis=1)
            strip += [jnp.where(own_block, tiled, 0.0), zero_rows]
        strip_scr[c] = jnp.concatenate(strip, axis=0).astype(bf16)

    conv_w = jnp.concatenate([cwx_ref[...], cwb_ref[...], cwc_ref[...]], axis=1)
    conv_b = jnp.concatenate([cbx_ref[...], cbb_ref[...], cbc_ref[...]], axis=1)
    dskip = dskip_ref[...]
    nw = nw_ref[...]
    lane_head = lax.broadcasted_iota(jnp.int32, (CHUNK, GROUP_WIDTH), 1) // SSD_HEAD_DIM
    contract0 = (((0,), (0,)), ((), ()))

    def local_pass(c, carry):
        base = pl.multiple_of(c * CHUNK, CHUNK)
        win = xp_scr[pl.ds(base, CHUNK + 2 * HALO), :]
        acc = jnp.broadcast_to(conv_b, (CHUNK, width))
        for k in range(SSD_CONV):
            acc = acc + _rows_ahead(win, HALO - SSD_CONV // 2 + k) * conv_w[k:k + 1, :]
        xbc = _silu_of_twice(acc)
        xbc_scr[pl.ds(base, CHUNK), :] = xbc.astype(bf16)
        xs = xbc[:, 0:GROUP_WIDTH]
        bm = xbc[:, GROUP_WIDTH:GROUP_WIDTH + SSD_STATE].astype(bf16)
        cm = xbc[:, GROUP_WIDTH + SSD_STATE:width].astype(bf16)
        y_scr[pl.ds(base, CHUNK), :] = xs * dskip
        col = col_scr[pl.ds(base, CHUNK), :]
        exp_scr[c] = jnp.dot(jnp.concatenate([col, eye], axis=1),
                             jnp.concatenate([selg_ref[...], strip_scr[c], mask_ref[...]], axis=0),
                             preferred_element_type=f32)
        cb_scr[c] = lax.dot_general(cm, bm, contract1, preferred_element_type=f32)
        state_w = jnp.dot(col, sels_ref[...], preferred_element_type=f32)
        weighted = jnp.concatenate([(xs * state_w[:, 0:GROUP_WIDTH]).astype(bf16),
                                    (xs * state_w[:, GROUP_WIDTH:]).astype(bf16)], axis=1)
        s_scr[c] = lax.dot_general(bm, weighted, contract0, preferred_element_type=f32)
        edge = jnp.concatenate([col[CHUNK - 16:CHUNK, :], col[0:16, :]], axis=0)
        edge_scr[c] = jnp.dot(edge, selc_ref[...], preferred_element_type=f32)
        return carry

    lax.fori_loop(0, n_chunks, local_pass, 0, unroll=8)

    def recur(i, carry):
        hf, hb = carry
        cf = i
        cr = n_chunks - 1 - i
        hin_scr[cf, :, 0:GROUP_WIDTH] = hf.astype(bf16)
        hin_scr[cr, :, GROUP_WIDTH:2 * GROUP_WIDTH] = hb.astype(bf16)
        hf = edge_scr[cf][15:16, 0:GROUP_WIDTH] * hf + s_scr[cf][:, 0:GROUP_WIDTH]
        hb = edge_scr[cr][16:17, GROUP_WIDTH:2 * GROUP_WIDTH] * hb + s_scr[cr][:, GROUP_WIDTH:2 * GROUP_WIDTH]
        return hf, hb

    h0 = jnp.zeros((SSD_STATE, GROUP_WIDTH), f32)
    lax.fori_loop(0, n_chunks, recur, (h0, h0))


    def output_pass(c, carry):
        base = pl.multiple_of(c * CHUNK, CHUNK)
        xbc = xbc_scr[pl.ds(base, CHUNK), :]
        xs = xbc[:, 0:GROUP_WIDTH]
        cm = xbc[:, GROUP_WIDTH + SSD_STATE:width]
        col = col_scr[pl.ds(base, CHUNK), :]
        carried = (jnp.dot(col, selc_ref[...], preferred_element_type=f32)
                   * jnp.dot(cm, hin_scr[c], preferred_element_type=f32))
        cb = cb_scr[c]
        g_heads, x_heads = [], []
        for r in range(HEADS_PER_GROUP):
            arg_f = exp_scr[c, :, r * CHUNK:(r + 1) * CHUNK]
            arg_b = exp_scr[c, :, (HEADS_PER_GROUP + r) * CHUNK:(HEADS_PER_GROUP + r + 1) * CHUNK]
            decay = jnp.exp2(arg_f) + jnp.exp2(arg_b)
            g_heads.append((cb * decay).astype(bf16))
            x_heads.append(jnp.where(lane_head == r, xs, jnp.zeros_like(xs)))
        y = y_scr[pl.ds(base, CHUNK), :] + jnp.dot(
            jnp.concatenate(g_heads, axis=1), jnp.concatenate(x_heads, axis=0),
            preferred_element_type=f32)
        y = y + carried[:, 0:GROUP_WIDTH] + carried[:, GROUP_WIDTH:]
        y = y * _silu_of_twice(z_ref[pl.ds(base, CHUNK), :].astype(f32))
        ms = jnp.mean(y * y, axis=-1, keepdims=True)
        o_ref[pl.ds(base, CHUNK), :] = (y * lax.rsqrt(ms + EPS) * nw).astype(o_ref.dtype)
        return carry

    lax.fori_loop(0, n_chunks, output_pass, 0, unroll=8)


def _ssd_core(proj, dt_t, dt_bias, a_log, conv_w, conv_b, dskip, norm_w):
    bsz, seq, _ = proj.shape
    e = SSD_GROUPS * GROUP_WIDTH
    width = GROUP_WIDTH + 2 * SSD_STATE
    n_x = e // GROUP_WIDTH
    xb = lambda blk, off: pl.BlockSpec((None, seq, blk), lambda b, g, off=off: (b, 0, off + g))
    in_specs = [
        xb(GROUP_WIDTH, 0),
        xb(GROUP_WIDTH, n_x),
        xb(SSD_STATE, 2 * e // SSD_STATE),
        xb(SSD_STATE, 2 * e // SSD_STATE + SSD_GROUPS),
        pl.BlockSpec((8, seq), lambda b, g: (g, b)),
        pl.BlockSpec((8, 1), lambda b, g: (g, 0)),
        pl.BlockSpec((8, 1), lambda b, g: (g, 0)),
        pl.BlockSpec((SSD_CONV, GROUP_WIDTH), lambda b, g: (0, g)),
        pl.BlockSpec((SSD_CONV, SSD_STATE), lambda b, g: (0, e // SSD_STATE + g)),
        pl.BlockSpec((SSD_CONV, SSD_STATE), lambda b, g: (0, e // SSD_STATE + SSD_GROUPS + g)),
        pl.BlockSpec((1, GROUP_WIDTH), lambda b, g: (0, g)),
        pl.BlockSpec((1, SSD_STATE), lambda b, g: (0, e // SSD_STATE + g)),
        pl.BlockSpec((1, SSD_STATE), lambda b, g: (0, e // SSD_STATE + SSD_GROUPS + g)),
        pl.BlockSpec((1, GROUP_WIDTH), lambda b, g: (0, g)),
        pl.BlockSpec((1, GROUP_WIDTH), lambda b, g: (0, g)),
    ]
    selectors = _selectors()
    in_specs += [pl.BlockSpec(s.shape, lambda b, g: (0, 0)) for s in selectors]
    return pl.pallas_call(
        _ssd_kernel, grid=(bsz, SSD_GROUPS), in_specs=in_specs,
        out_specs=pl.BlockSpec((None, seq, GROUP_WIDTH), lambda b, g: (b, 0, g)),
        out_shape=jax.ShapeDtypeStruct((bsz, seq, e), bf16),
        scratch_shapes=[
            pltpu.VMEM((seq + 2 * HALO, width), f32),
            pltpu.VMEM((seq, width), bf16),
            pltpu.VMEM((seq // CHUNK, _STRIP_ROWS, 2 * HEADS_PER_GROUP * CHUNK), bf16),
            pltpu.VMEM((seq, CHUNK), bf16),
            pltpu.VMEM((seq, GROUP_WIDTH), f32),
            pltpu.VMEM((seq // CHUNK, SSD_STATE, 2 * GROUP_WIDTH), f32),
            pltpu.VMEM((seq // CHUNK, 32, 2 * GROUP_WIDTH), f32),
            pltpu.VMEM((seq // CHUNK, SSD_STATE, 2 * GROUP_WIDTH), bf16),
            pltpu.VMEM((seq // CHUNK, CHUNK, 2 * HEADS_PER_GROUP * CHUNK), f32),
            pltpu.VMEM((seq // CHUNK, CHUNK, CHUNK), f32),
        ],
        compiler_params=pltpu.CompilerParams(dimension_semantics=("arbitrary", "arbitrary"),
                                             vmem_limit_bytes=VMEM_LIMIT),
        name="ssd_core")(proj, proj, proj, proj, dt_t, dt_bias, a_log,
                         conv_w, conv_w, conv_w, conv_b, conv_b, conv_b, dskip, norm_w,
                         *selectors)


def _window_sum(win, w):
    half = w // 2
    if w == 2:
        return _rows_ahead(win, HALO - 1) + _rows_ahead(win, HALO)
    t = win + pltpu.roll(win, win.shape[0] - 1, axis=0)
    span = 2
    while span * 2 < w:
        t = t + pltpu.roll(t, win.shape[0] - span, axis=0)
        span *= 2
    return _rows_ahead(t, HALO - half) + _rows_ahead(t, HALO)


def _pool_kernel(v_ref, gate_ref, mix_ref, scale_ref, o_ref, vp_scr):
    seq, gd = v_ref.shape
    n_steps = seq // POOL_ROWS
    gi = pl.program_id(1)

    vp_scr[0:HALO, :] = jnp.zeros((HALO, gd), f32)
    vp_scr[seq + HALO:seq + 2 * HALO, :] = jnp.zeros((HALO, gd), f32)

    def fill(c, carry):
        base = pl.multiple_of(c * POOL_ROWS, POOL_ROWS)
        vp_scr[pl.ds(base + HALO, POOL_ROWS), :] = v_ref[pl.ds(base, POOL_ROWS), :].astype(f32)
        return carry

    lax.fori_loop(0, n_steps, fill, 0)
    scale = scale_ref[...]

    for k, w in enumerate(POOL_WINDOWS):
        @pl.when(gi == k)
        def _(w=w):
            half = w // 2

            def step(c, carry):
                base = pl.multiple_of(c * POOL_ROWS, POOL_ROWS)
                win = vp_scr[pl.ds(base, POOL_ROWS + 2 * HALO), :]
                pos = base + lax.broadcasted_iota(jnp.int32, (POOL_ROWS, 1), 0)
                cnt = (jnp.minimum(pos + half, seq) - jnp.maximum(pos - half, 0)).astype(f32)
                pooled = _window_sum(win, w) / cnt - win[HALO:HALO + POOL_ROWS, :]
                mixed = jnp.dot(pooled.astype(bf16), mix_ref[...], preferred_element_type=f32)
                gate = gate_ref[pl.ds(base, POOL_ROWS), :].astype(f32)
                o_ref[pl.ds(base, POOL_ROWS), :] = (mixed * scale * _silu_of_twice(gate)).astype(o_ref.dtype)
                return carry

            lax.fori_loop(0, n_steps, step, 0)


def _pool_core(proj, mix_w, scale):
    bsz, seq, e2 = proj.shape
    e = e2 // 2
    ng = len(POOL_WINDOWS)
    gd = e // ng
    return pl.pallas_call(
        _pool_kernel, grid=(bsz, ng),
        in_specs=[pl.BlockSpec((None, seq, gd), lambda b, g: (b, 0, g)),
                  pl.BlockSpec((None, seq, gd), lambda b, g: (b, 0, ng + g)),
                  pl.BlockSpec((None, gd, gd), lambda b, g: (g, 0, 0)),
                  pl.BlockSpec((1, gd), lambda b, g: (0, g))],
        out_specs=pl.BlockSpec((None, seq, gd), lambda b, g: (b, 0, g)),
        out_shape=jax.ShapeDtypeStruct((bsz, seq, e), bf16),
        scratch_shapes=[pltpu.VMEM((seq + 2 * HALO, gd), f32)],
        compiler_params=pltpu.CompilerParams(dimension_semantics=("arbitrary", "arbitrary"),
                                             vmem_limit_bytes=VMEM_LIMIT),
        name="pool_core")(proj, proj, mix_w, scale)


def _group_major(p):
    return p.reshape(2, SSD_GROUPS, HEADS_PER_GROUP).transpose(1, 0, 2).reshape(-1, 1)


def kernel(x, norm_w, ssd_w_in, ssd_conv_w, ssd_conv_b, ssd_dt_bias, ssd_a_log, ssd_d, ssd_norm_w, ssd_w_out, pool_w_in, pool_mix_w, pool_scale, pool_w_out, final_norm_w):
    bsz, seq, d = x.shape
    t = bsz * seq
    depth = norm_w.shape[0]
    e = ssd_w_out.shape[1]
    n_main = 2 * e + 2 * SSD_GROUPS * SSD_STATE
    h = x.reshape(t, d)
    for i in range(depth):
        j = i // 2
        nw = norm_w[i].reshape(1, d)
        final_nw = final_norm_w.reshape(1, d) if i == depth - 1 else None
        if i % 2 == 0:
            z_half = jnp.where(jnp.arange(n_main) < e, 0.5, 1.0).astype(f32)
            w_main = (ssd_w_in[j][:, :n_main] * z_half).astype(bf16)
            w_dt = ssd_w_in[j][:, n_main:]
            wdt_t = (w_dt.reshape(d, 2, SSD_GROUPS, HEADS_PER_GROUP).transpose(2, 1, 3, 0)
                     .reshape(-1, d).astype(bf16))
            proj, dt_t = _in_proj(h, nw, w_main, wdt_t)
            y = _ssd_core(proj.reshape(bsz, seq, n_main), dt_t,
                          _group_major(ssd_dt_bias[j]), _group_major(ssd_a_log[j]),
                          0.5 * ssd_conv_w[j], 0.5 * ssd_conv_b[j].reshape(1, -1),
                          jnp.repeat(ssd_d[j], SSD_HEAD_DIM).reshape(1, e),
                          ssd_norm_w[j].reshape(1, e))
            h = _out_proj(y.reshape(t, e), ssd_w_out[j].astype(bf16), h, final_nw)
        else:
            gate_half = jnp.where(jnp.arange(2 * e) < e, 1.0, 0.5).astype(f32)
            proj = _in_proj(h, nw, (pool_w_in[j] * gate_half).astype(bf16))
            y = _pool_core(proj.reshape(bsz, seq, 2 * e), pool_mix_w[j].astype(bf16),
                           pool_scale[j].reshape(1, e))
            h = _out_proj(y.reshape(t, e), pool_w_out[j].astype(bf16), h, final_nw)
    return h.reshape(bsz, seq, d)
```

```python
import functools

import jax
import jax.numpy as jnp
import numpy as np
from jax import lax
from jax.experimental import pallas as pl
from jax.experimental.pallas import tpu as pltpu

f32 = jnp.float32
bf16 = jnp.bfloat16

EPS = 1e-6
LANES = 128
LOG2E = 1.4426950408889634
SSD_HEAD_DIM = 64
SSD_GROUPS = 8
SSD_STATE = 128
SSD_CONV = 5
HEADS_PER_GROUP = 4
GROUP_WIDTH = HEADS_PER_GROUP * SSD_HEAD_DIM
CHUNK = 128
HALO = 8
POOL_WINDOWS = (2, 4, 8, 16)
POOL_ROWS = 256
PASS_UNROLL = 16

VMEM_LIMIT = 48 * 1024 * 1024


def _silu_of_twice(half):
    return half + half * jnp.tanh(half)


def _rows_ahead(win, k):
    n = win.shape[0]
    if k % 8 == 0:
        return win[k:k + n - 2 * HALO, :]
    return pltpu.roll(win, n - k, axis=0)[0:n - 2 * HALO, :]


def _stage_rows(h_ref, u_scr, rs_scr):
    xv = h_ref[...]
    rs = lax.rsqrt(jnp.mean(xv * xv, axis=-1, keepdims=True) + EPS)
    rs_scr[...] = jnp.broadcast_to(rs, rs_scr.shape)
    u_scr[...] = xv.astype(bf16)


def _scaled_dot(u_scr, rs_scr, w):
    acc = jnp.dot(u_scr[...], w, preferred_element_type=f32)
    return acc * jnp.concatenate([rs_scr[...]] * (acc.shape[1] // LANES), axis=1)


def _in_proj_kernel(h_ref, w_ref, o_ref, u_scr, rs_scr):
    @pl.when(pl.program_id(1) == 0)
    def _():
        _stage_rows(h_ref, u_scr, rs_scr)

    o_ref[...] = _scaled_dot(u_scr, rs_scr, w_ref[...]).astype(o_ref.dtype)


def _in_proj_dt_kernel(h_ref, w_ref, wdt_ref, o_ref, dt_ref, u_scr, rs_scr):
    @pl.when(pl.program_id(1) == 0)
    def _():
        _stage_rows(h_ref, u_scr, rs_scr)
        dt_ref[...] = _scaled_dot(u_scr, rs_scr, wdt_ref[...]).T[0:dt_ref.shape[0], :]

    o_ref[...] = _scaled_dot(u_scr, rs_scr, w_ref[...]).astype(o_ref.dtype)


def _in_proj(h, w, wdt=None, *, tm=1024, tn=1024):
    t, d = h.shape
    n = w.shape[1]
    grid = (t // tm, n // tn)
    h_spec = pl.BlockSpec((tm, d), lambda i, j: (i, 0))
    w_spec = pl.BlockSpec((d, tn), lambda i, j: (0, j))
    o_spec = pl.BlockSpec((tm, tn), lambda i, j: (i, j))
    params = pltpu.CompilerParams(dimension_semantics=("arbitrary", "arbitrary"),
                                  vmem_limit_bytes=VMEM_LIMIT)
    scratch = [pltpu.VMEM((tm, d), bf16), pltpu.VMEM((tm, LANES), f32)]
    if wdt is None:
        return pl.pallas_call(
            _in_proj_kernel, grid=grid, in_specs=[h_spec, w_spec], out_specs=o_spec,
            out_shape=jax.ShapeDtypeStruct((t, n), bf16), scratch_shapes=scratch,
            compiler_params=params, name="in_proj")(h, w)
    nd = 2 * SSD_GROUPS * HEADS_PER_GROUP
    return pl.pallas_call(
        _in_proj_dt_kernel, grid=grid,
        in_specs=[h_spec, w_spec, pl.BlockSpec(wdt.shape, lambda i, j: (0, 0))],
        out_specs=[o_spec, pl.BlockSpec((nd, tm), lambda i, j: (0, i))],
        out_shape=[jax.ShapeDtypeStruct((t, n), bf16), jax.ShapeDtypeStruct((nd, t), f32)],
        scratch_shapes=scratch, compiler_params=params, name="in_proj_dt")(h, w, wdt)


def _out_proj_kernel(y_ref, w_ref, h_ref, o_ref):
    o_ref[...] = h_ref[...] + jnp.dot(y_ref[...], w_ref[...], preferred_element_type=f32)


def _out_proj_final_kernel(y_ref, w_ref, h_ref, nw_ref, o_ref):
    acc = h_ref[...] + jnp.dot(y_ref[...], w_ref[...], preferred_element_type=f32)
    ms = jnp.mean(acc * acc, axis=-1, keepdims=True)
    o_ref[...] = acc * lax.rsqrt(ms + EPS) * nw_ref[...]


def _out_proj(y, w, h, final_nw=None, *, tm=1024):
    t, e = y.shape
    d = w.shape[1]
    in_specs = [pl.BlockSpec((tm, e), lambda i: (i, 0)),
                pl.BlockSpec((e, d), lambda i: (0, 0)),
                pl.BlockSpec((tm, d), lambda i: (i, 0))]
    args = [y, w, h]
    body = _out_proj_kernel
    if final_nw is not None:
        in_specs.append(pl.BlockSpec((1, d), lambda i: (0, 0)))
        args.append(final_nw)
        body = _out_proj_final_kernel
    return pl.pallas_call(
        body, grid=(t // tm,), in_specs=in_specs,
        out_specs=pl.BlockSpec((tm, d), lambda i: (i, 0)),
        out_shape=jax.ShapeDtypeStruct((t, d), f32),
        compiler_params=pltpu.CompilerParams(dimension_semantics=("arbitrary",),
                                             vmem_limit_bytes=VMEM_LIMIT),
        name="out_proj")(*args)


def _split3(v):
    hi = v.astype(bf16).astype(f32)
    rem = v - hi
    mid = rem.astype(bf16).astype(f32)
    return hi, mid, rem - mid


_N_VALS = 24
_SPLIT_ROWS = 3 * _N_VALS
_STRIP_BASE = 80
_STRIP_ROWS = 48
_MASKED = -1e30


def _selectors():
    n_dir = 2 * HEADS_PER_GROUP
    sel_g = np.zeros((_STRIP_BASE, n_dir * CHUNK), np.float32)
    for j in range(n_dir):
        for p in range(3):
            sel_g[p * _N_VALS + j, j * CHUNK:(j + 1) * CHUNK] = 1.0 if j < HEADS_PER_GROUP else -1.0
    sels = []
    for fwd_v, bwd_v in ((8, 12), (16, 20)):
        s = np.zeros((CHUNK, 2 * GROUP_WIDTH), np.float32)
        for a, v0 in enumerate((fwd_v, bwd_v)):
            for r in range(HEADS_PER_GROUP):
                lo = a * GROUP_WIDTH + r * SSD_HEAD_DIM
                for p in range(3):
                    s[p * _N_VALS + v0 + r, lo:lo + SSD_HEAD_DIM] = 1.0
        sels.append(s)
    li = np.arange(CHUNK)[:, None]
    si = np.arange(CHUNK)[None, :]
    mask = np.concatenate([np.where(si > li, _MASKED, 0.0)] * HEADS_PER_GROUP
                          + [np.where(si < li, _MASKED, 0.0)] * HEADS_PER_GROUP, axis=1)
    return (jnp.asarray(sel_g, bf16), jnp.asarray(mask, bf16),
            jnp.asarray(sels[0], bf16), jnp.asarray(sels[1], bf16))


def _ssd_kernel(z_ref, x_ref, b_ref, c_ref, dt_ref, dtb_ref, alog_ref,
                cwx_ref, cwb_ref, cwc_ref, cbx_ref, cbb_ref, cbc_ref, dskip_ref, nw_ref,
                selg_ref, mask_ref, sels_ref, selc_ref,
                o_ref, xp_scr, xbc_scr, strip_scr, col_scr, y_scr, s_scr, edge_scr, hin_scr,
                exp_scr, cb_scr):
    seq = x_ref.shape[0]
    n_chunks = seq // CHUNK
    width = GROUP_WIDTH + 2 * SSD_STATE

    xp_scr[0:HALO, :] = jnp.zeros((HALO, width), f32)
    xp_scr[seq + HALO:seq + 2 * HALO, :] = jnp.zeros((HALO, width), f32)

    for c in range(n_chunks):
        src_rows = slice(c * CHUNK, (c + 1) * CHUNK)
        rows = slice(c * CHUNK + HALO, (c + 1) * CHUNK + HALO)
        xp_scr[rows, 0:GROUP_WIDTH] = x_ref[src_rows, :].astype(f32)
        xp_scr[rows, GROUP_WIDTH:GROUP_WIDTH + SSD_STATE] = b_ref[src_rows, :].astype(f32)
        xp_scr[rows, GROUP_WIDTH + SSD_STATE:width] = c_ref[src_rows, :].astype(f32)

    n_dir = 2 * HEADS_PER_GROUP
    stack = lambda v: jnp.concatenate([v] * n_chunks, axis=0)
    raw = (jnp.concatenate([dt_ref[:, c * CHUNK:(c + 1) * CHUNK] for c in range(n_chunks)], axis=0)
           + stack(dtb_ref[...]))
    is_fwd = (lax.broadcasted_iota(jnp.int32, raw.shape, 0) & (n_dir - 1)) < HEADS_PER_GROUP
    dt = jnp.maximum(raw, 0.0) + jnp.log1p(jnp.exp(-jnp.abs(raw)))
    a2 = -(dt * stack(jnp.exp(alog_ref[...]) * LOG2E))
    tri = (lax.broadcasted_iota(jnp.int32, (CHUNK, CHUNK), 0)
           <= lax.broadcasted_iota(jnp.int32, (CHUNK, CHUNK), 1)).astype(bf16)
    sums = jnp.dot(jnp.concatenate(_split3(a2), axis=0).astype(bf16), tri, preferred_element_type=f32)
    n_rows = raw.shape[0]
    incl = sums[0:n_rows] + sums[n_rows:2 * n_rows] + sums[2 * n_rows:3 * n_rows]
    tot = incl[:, CHUNK - 1:CHUNK]
    cs = jnp.where(is_fwd, incl, incl - a2)
    log2_dt = jnp.maximum(jnp.log2(dt), -1e30)
    src_parts = _split3(jnp.where(is_fwd, log2_dt - cs, cs + log2_dt))
    to_end = jnp.exp2(tot - cs)
    from_start = jnp.exp2(cs)
    state_w = jnp.where(is_fwd, to_end, from_start) * dt
    carry_w = jnp.where(is_fwd, from_start, to_end)
    val_parts = list(zip(_split3(cs), _split3(state_w), _split3(carry_w)))

    own_block = (lax.broadcasted_iota(jnp.int32, (8, n_dir * CHUNK), 1) // CHUNK
                 == lax.broadcasted_iota(jnp.int32, (8, n_dir * CHUNK), 0))
    eye = (lax.broadcasted_iota(jnp.int32, (CHUNK, CHUNK), 0)
           == lax.broadcasted_iota(jnp.int32, (CHUNK, CHUNK), 1)).astype(bf16)
    ones_rows = jnp.ones((CHUNK - _SPLIT_ROWS, CHUNK), f32)
    zero_rows = jnp.zeros((8, n_dir * CHUNK), f32)
    contract1 = (((1,), (1,)), ((), ()))
    for c in range(n_chunks):
        own = slice(c * n_dir, (c + 1) * n_dir)
        rows = jnp.concatenate([v[own] for part in val_parts for v in part] + [ones_rows],
                               axis=0).astype(bf16)
        col_scr[c * CHUNK:(c + 1) * CHUNK, :] = lax.dot_general(
            eye, rows, contract1, preferred_element_type=f32).astype(bf16)
        strip = []
        for part in src_parts:
            tiled = jnp.concatenate([part[own]] * n_dir, axis=1)
            strip += [jnp.where(own_block, tiled, 0.0), zero_rows]
        strip_scr[c] = jnp.concatenate(strip, axis=0).astype(bf16)

    conv_w = jnp.concatenate([cwx_ref[...], cwb_ref[...], cwc_ref[...]], axis=1)
    conv_b = jnp.concatenate([cbx_ref[...], cbb_ref[...], cbc_ref[...]], axis=1)
    dskip = dskip_ref[...]
    nw = nw_ref[...]
    lane_head = lax.broadcasted_iota(jnp.int32, (CHUNK, GROUP_WIDTH), 1) // SSD_HEAD_DIM
    contract0 = (((0,), (0,)), ((), ()))

    def local_pass(c, carry):
        base = pl.multiple_of(c * CHUNK, CHUNK)
        win = xp_scr[pl.ds(base, CHUNK + 2 * HALO), :]
        acc = jnp.broadcast_to(conv_b, (CHUNK, width))
        for k in range(SSD_CONV):
            acc = acc + _rows_ahead(win, HALO - SSD_CONV // 2 + k) * conv_w[k:k + 1, :]
        xbc = _silu_of_twice(acc)
        xbc_scr[pl.ds(base, CHUNK), :] = xbc.astype(bf16)
        xs = xbc[:, 0:GROUP_WIDTH]
        bm = xbc[:, GROUP_WIDTH:GROUP_WIDTH + SSD_STATE].astype(bf16)
        cm = xbc[:, GROUP_WIDTH + SSD_STATE:width].astype(bf16)
        y_scr[pl.ds(base, CHUNK), :] = xs * dskip
        col = col_scr[pl.ds(base, CHUNK), :]
        exp_scr[c] = jnp.dot(jnp.concatenate([col, eye], axis=1),
                             jnp.concatenate([selg_ref[...], strip_scr[c], mask_ref[...]], axis=0),
                             preferred_element_type=f32)
        cb_scr[c] = lax.dot_general(cm, bm, contract1, preferred_element_type=f32)
        state_w = jnp.dot(col, sels_ref[...], preferred_element_type=f32)
        weighted = jnp.concatenate([(xs * state_w[:, 0:GROUP_WIDTH]).astype(bf16),
                                    (xs * state_w[:, GROUP_WIDTH:]).astype(bf16)], axis=1)
        s_scr[c] = lax.dot_general(bm, weighted, contract0, preferred_element_type=f32)
        edge = jnp.concatenate([col[CHUNK - 16:CHUNK, :], col[0:16, :]], axis=0)
        edge_scr[c] = jnp.dot(edge, selc_ref[...], preferred_element_type=f32)
        return carry

    lax.fori_loop(0, n_chunks, local_pass, 0, unroll=PASS_UNROLL)

    def recur(i, carry):
        hf, hb = carry
        cf = i
        cr = n_chunks - 1 - i
        hin_scr[cf, :, 0:GROUP_WIDTH] = hf.astype(bf16)
        hin_scr[cr, :, GROUP_WIDTH:2 * GROUP_WIDTH] = hb.astype(bf16)
        hf = edge_scr[cf][15:16, 0:GROUP_WIDTH] * hf + s_scr[cf][:, 0:GROUP_WIDTH]
        hb = edge_scr[cr][16:17, GROUP_WIDTH:2 * GROUP_WIDTH] * hb + s_scr[cr][:, GROUP_WIDTH:2 * GROUP_WIDTH]
        return hf, hb

    h0 = jnp.zeros((SSD_STATE, GROUP_WIDTH), f32)
    lax.fori_loop(0, n_chunks, recur, (h0, h0))


    def output_pass(c, carry):
        base = pl.multiple_of(c * CHUNK, CHUNK)
        xbc = xbc_scr[pl.ds(base, CHUNK), :]
        xs = xbc[:, 0:GROUP_WIDTH]
        cm = xbc[:, GROUP_WIDTH + SSD_STATE:width]
        col = col_scr[pl.ds(base, CHUNK), :]
        carried = (jnp.dot(col, selc_ref[...], preferred_element_type=f32)
                   * jnp.dot(cm, hin_scr[c], preferred_element_type=f32))
        cb = cb_scr[c]
        g_heads, x_heads = [], []
        for r in range(HEADS_PER_GROUP):
            arg_f = exp_scr[c, :, r * CHUNK:(r + 1) * CHUNK]
            arg_b = exp_scr[c, :, (HEADS_PER_GROUP + r) * CHUNK:(HEADS_PER_GROUP + r + 1) * CHUNK]
            decay = jnp.exp2(arg_f) + jnp.exp2(arg_b)
            g_heads.append((cb * decay).astype(bf16))
            x_heads.append(jnp.where(lane_head == r, xs, jnp.zeros_like(xs)))
        y = y_scr[pl.ds(base, CHUNK), :] + jnp.dot(
            jnp.concatenate(g_heads, axis=1), jnp.concatenate(x_heads, axis=0),
            preferred_element_type=f32)
        y = y + carried[:, 0:GROUP_WIDTH] + carried[:, GROUP_WIDTH:]
        y = y * _silu_of_twice(z_ref[pl.ds(base, CHUNK), :].astype(f32))
        ms = jnp.mean(y * y, axis=-1, keepdims=True)
        o_ref[pl.ds(base, CHUNK), :] = (y * lax.rsqrt(ms + EPS) * nw).astype(o_ref.dtype)
        return carry

    lax.fori_loop(0, n_chunks, output_pass, 0, unroll=PASS_UNROLL)


def _ssd_core(proj, dt_t, dt_bias, a_log, conv_w, conv_b, dskip, norm_w):
    bsz, seq, _ = proj.shape
    e = SSD_GROUPS * GROUP_WIDTH
    width = GROUP_WIDTH + 2 * SSD_STATE
    n_x = e // GROUP_WIDTH
    xb = lambda blk, off: pl.BlockSpec((None, seq, blk), lambda b, g, off=off: (b, 0, off + g))
    in_specs = [
        xb(GROUP_WIDTH, 0),
        xb(GROUP_WIDTH, n_x),
        xb(SSD_STATE, 2 * e // SSD_STATE),
        xb(SSD_STATE, 2 * e // SSD_STATE + SSD_GROUPS),
        pl.BlockSpec((8, seq), lambda b, g: (g, b)),
        pl.BlockSpec((8, 1), lambda b, g: (g, 0)),
        pl.BlockSpec((8, 1), lambda b, g: (g, 0)),
        pl.BlockSpec((SSD_CONV, GROUP_WIDTH), lambda b, g: (0, g)),
        pl.BlockSpec((SSD_CONV, SSD_STATE), lambda b, g: (0, e // SSD_STATE + g)),
        pl.BlockSpec((SSD_CONV, SSD_STATE), lambda b, g: (0, e // SSD_STATE + SSD_GROUPS + g)),
        pl.BlockSpec((1, GROUP_WIDTH), lambda b, g: (0, g)),
        pl.BlockSpec((1, SSD_STATE), lambda b, g: (0, e // SSD_STATE + g)),
        pl.BlockSpec((1, SSD_STATE), lambda b, g: (0, e // SSD_STATE + SSD_GROUPS + g)),
        pl.BlockSpec((1, GROUP_WIDTH), lambda b, g: (0, g)),
        pl.BlockSpec((1, GROUP_WIDTH), lambda b, g: (0, g)),
    ]
    selectors = _selectors()
    in_specs += [pl.BlockSpec(s.shape, lambda b, g: (0, 0)) for s in selectors]
    return pl.pallas_call(
        _ssd_kernel, grid=(bsz, SSD_GROUPS), in_specs=in_specs,
        out_specs=pl.BlockSpec((None, seq, GROUP_WIDTH), lambda b, g: (b, 0, g)),
        out_shape=jax.ShapeDtypeStruct((bsz, seq, e), bf16),
        scratch_shapes=[
            pltpu.VMEM((seq + 2 * HALO, width), f32),
            pltpu.VMEM((seq, width), bf16),
            pltpu.VMEM((seq // CHUNK, _STRIP_ROWS, 2 * HEADS_PER_GROUP * CHUNK), bf16),
            pltpu.VMEM((seq, CHUNK), bf16),
            pltpu.VMEM((seq, GROUP_WIDTH), f32),
            pltpu.VMEM((seq // CHUNK, SSD_STATE, 2 * GROUP_WIDTH), f32),
            pltpu.VMEM((seq // CHUNK, 32, 2 * GROUP_WIDTH), f32),
            pltpu.VMEM((seq // CHUNK, SSD_STATE, 2 * GROUP_WIDTH), bf16),
            pltpu.VMEM((seq // CHUNK, CHUNK, 2 * HEADS_PER_GROUP * CHUNK), f32),
            pltpu.VMEM((seq // CHUNK, CHUNK, CHUNK), f32),
        ],
        compiler_params=pltpu.CompilerParams(dimension_semantics=("arbitrary", "arbitrary"),
                                             vmem_limit_bytes=VMEM_LIMIT),
        name="ssd_core")(proj, proj, proj, proj, dt_t, dt_bias, a_log,
                         conv_w, conv_w, conv_w, conv_b, conv_b, conv_b, dskip, norm_w,
                         *selectors)


def _window_sum(win, w):
    half = w // 2
    if w == 2:
        return _rows_ahead(win, HALO - 1) + _rows_ahead(win, HALO)
    t = win + pltpu.roll(win, win.shape[0] - 1, axis=0)
    span = 2
    while span * 2 < w:
        t = t + pltpu.roll(t, win.shape[0] - span, axis=0)
        span *= 2
    return _rows_ahead(t, HALO - half) + _rows_ahead(t, HALO)


def _pool_kernel(v_ref, gate_ref, mix_ref, scale_ref, o_ref, vp_scr):
    seq, gd = v_ref.shape
    n_steps = seq // POOL_ROWS
    gi = pl.program_id(1)

    vp_scr[0:HALO, :] = jnp.zeros((HALO, gd), f32)
    vp_scr[seq + HALO:seq + 2 * HALO, :] = jnp.zeros((HALO, gd), f32)

    def fill(c, carry):
        base = pl.multiple_of(c * POOL_ROWS, POOL_ROWS)
        vp_scr[pl.ds(base + HALO, POOL_ROWS), :] = v_ref[pl.ds(base, POOL_ROWS), :].astype(f32)
        return carry

    lax.fori_loop(0, n_steps, fill, 0)
    scale = scale_ref[...]

    for k, w in enumerate(POOL_WINDOWS):
        @pl.when(gi == k)
        def _(w=w):
            half = w // 2

            def step(c, carry):
                base = pl.multiple_of(c * POOL_ROWS, POOL_ROWS)
                win = vp_scr[pl.ds(base, POOL_ROWS + 2 * HALO), :]
                pos = base + lax.broadcasted_iota(jnp.int32, (POOL_ROWS, 1), 0)
                cnt = (jnp.minimum(pos + half, seq) - jnp.maximum(pos - half, 0)).astype(f32)
                pooled = _window_sum(win, w) / cnt - win[HALO:HALO + POOL_ROWS, :]
                mixed = jnp.dot(pooled.astype(bf16), mix_ref[...], preferred_element_type=f32)
                gate = gate_ref[pl.ds(base, POOL_ROWS), :].astype(f32)
                o_ref[pl.ds(base, POOL_ROWS), :] = (mixed * scale * _silu_of_twice(gate)).astype(o_ref.dtype)
                return carry

            lax.fori_loop(0, n_steps, step, 0)


def _pool_core(proj, mix_w, scale):
    bsz, seq, e2 = proj.shape
    e = e2 // 2
    ng = len(POOL_WINDOWS)
    gd = e // ng
    return pl.pallas_call(
        _pool_kernel, grid=(bsz, ng),
        in_specs=[pl.BlockSpec((None, seq, gd), lambda b, g: (b, 0, g)),
                  pl.BlockSpec((None, seq, gd), lambda b, g: (b, 0, ng + g)),
                  pl.BlockSpec((None, gd, gd), lambda b, g: (g, 0, 0)),
                  pl.BlockSpec((1, gd), lambda b, g: (0, g))],
        out_specs=pl.BlockSpec((None, seq, gd), lambda b, g: (b, 0, g)),
        out_shape=jax.ShapeDtypeStruct((bsz, seq, e), bf16),
        scratch_shapes=[pltpu.VMEM((seq + 2 * HALO, gd), f32)],
        compiler_params=pltpu.CompilerParams(dimension_semantics=("arbitrary", "arbitrary"),
                                             vmem_limit_bytes=VMEM_LIMIT),
        name="pool_core")(proj, proj, mix_w, scale)


def _group_major(p):
    return p.reshape(2, SSD_GROUPS, HEADS_PER_GROUP).transpose(1, 0, 2).reshape(-1, 1)


def kernel(x, norm_w, ssd_w_in, ssd_conv_w, ssd_conv_b, ssd_dt_bias, ssd_a_log, ssd_d, ssd_norm_w, ssd_w_out, pool_w_in, pool_mix_w, pool_scale, pool_w_out, final_norm_w):
    bsz, seq, d = x.shape
    t = bsz * seq
    depth = norm_w.shape[0]
    e = ssd_w_out.shape[1]
    n_main = 2 * e + 2 * SSD_GROUPS * SSD_STATE
    h = x.reshape(t, d)
    for i in range(depth):
        j = i // 2
        gain = norm_w[i].reshape(d, 1)
        final_nw = final_norm_w.reshape(1, d) if i == depth - 1 else None
        if i % 2 == 0:
            z_half = jnp.where(jnp.arange(n_main) < e, 0.5, 1.0).astype(f32)
            w_main = (ssd_w_in[j][:, :n_main] * (gain * z_half)).astype(bf16)
            w_dt = ssd_w_in[j][:, n_main:] * gain
            w_dt = (w_dt.reshape(d, 2, SSD_GROUPS, HEADS_PER_GROUP).transpose(0, 2, 1, 3)
                    .reshape(d, -1))
            w_dt = jnp.pad(w_dt, ((0, 0), (0, LANES - w_dt.shape[1]))).astype(bf16)
            proj, dt_t = _in_proj(h, w_main, w_dt)
            y = _ssd_core(proj.reshape(bsz, seq, n_main), dt_t,
                          _group_major(ssd_dt_bias[j]), _group_major(ssd_a_log[j]),
                          0.5 * ssd_conv_w[j], 0.5 * ssd_conv_b[j].reshape(1, -1),
                          jnp.repeat(ssd_d[j], SSD_HEAD_DIM).reshape(1, e),
                          ssd_norm_w[j].reshape(1, e))
            h = _out_proj(y.reshape(t, e), ssd_w_out[j].astype(bf16), h, final_nw)
        else:
            gate_half = jnp.where(jnp.arange(2 * e) < e, 1.0, 0.5).astype(f32)
            proj = _in_proj(h, (pool_w_in[j] * (gain * gate_half)).astype(bf16))
            y = _pool_core(proj.reshape(bsz, seq, 2 * e), pool_mix_w[j].astype(bf16),
                           pool_scale[j].reshape(1, e))
            h = _out_proj(y.reshape(t, e), pool_w_out[j].astype(bf16), h, final_nw)
    return h.reshape(bsz, seq, d)
```

```python
import functools

import jax
import jax.numpy as jnp
import numpy as np
from jax import lax
from jax.experimental import pallas as pl
from jax.experimental.pallas import tpu as pltpu

f32 = jnp.float32
bf16 = jnp.bfloat16

EPS = 1e-6
LANES = 128
LOG2E = 1.4426950408889634
SSD_HEAD_DIM = 64
SSD_GROUPS = 8
SSD_STATE = 128
SSD_CONV = 5
HEADS_PER_GROUP = 4
GROUP_WIDTH = HEADS_PER_GROUP * SSD_HEAD_DIM
CHUNK = 128
HALO = 8
POOL_WINDOWS = (2, 4, 8, 16)
POOL_ROWS = 256
PROJ_ROWS = 256

VMEM_LIMIT = 48 * 1024 * 1024
SSD_VMEM_LIMIT = 58 * 1024 * 1024


def _silu_of_twice(half):
    return half + half * jnp.tanh(half)


def _rows_ahead(win, k):
    n = win.shape[0]
    if k % 8 == 0:
        return win[k:k + n - 2 * HALO, :]
    return pltpu.roll(win, n - k, axis=0)[0:n - 2 * HALO, :]


def _normed(h_ref, nw_ref):
    xv = h_ref[...]
    ms = jnp.mean(xv * xv, axis=-1, keepdims=True)
    return (xv * lax.rsqrt(ms + EPS) * nw_ref[...]).astype(bf16)


def _in_proj_kernel(h_ref, nw_ref, w_ref, o_ref, u_scr):
    @pl.when(pl.program_id(1) == 0)
    def _():
        u_scr[...] = _normed(h_ref, nw_ref)

    o_ref[...] = jnp.dot(u_scr[...], w_ref[...], preferred_element_type=f32).astype(o_ref.dtype)


def _in_proj(h, nw, w, *, tm=1024, tn=1024):
    t, d = h.shape
    n = w.shape[1]
    return pl.pallas_call(
        _in_proj_kernel, grid=(t // tm, n // tn),
        in_specs=[pl.BlockSpec((tm, d), lambda i, j: (i, 0)),
                  pl.BlockSpec((1, d), lambda i, j: (0, 0)),
                  pl.BlockSpec((d, tn), lambda i, j: (0, j))],
        out_specs=pl.BlockSpec((tm, tn), lambda i, j: (i, j)),
        out_shape=jax.ShapeDtypeStruct((t, n), bf16),
        scratch_shapes=[pltpu.VMEM((tm, d), bf16)],
        compiler_params=pltpu.CompilerParams(dimension_semantics=("arbitrary", "arbitrary"),
                                             vmem_limit_bytes=VMEM_LIMIT),
        name="in_proj")(h, nw, w)


def _norm_dt_kernel(h_ref, nw_ref, wdt_ref, u_ref, dt_ref):
    u = _normed(h_ref, nw_ref)
    u_ref[...] = u
    dt_ref[...] = lax.dot_general(wdt_ref[...], u, (((1,), (1,)), ((), ())),
                                  preferred_element_type=f32)


def _norm_dt(h, nw, wdt_t, *, tm=1024):
    t, d = h.shape
    nd = wdt_t.shape[0]
    return pl.pallas_call(
        _norm_dt_kernel, grid=(t // tm,),
        in_specs=[pl.BlockSpec((tm, d), lambda i: (i, 0)),
                  pl.BlockSpec((1, d), lambda i: (0, 0)),
                  pl.BlockSpec((nd, d), lambda i: (0, 0))],
        out_specs=[pl.BlockSpec((tm, d), lambda i: (i, 0)), pl.BlockSpec((nd, tm), lambda i: (0, i))],
        out_shape=[jax.ShapeDtypeStruct((t, d), bf16), jax.ShapeDtypeStruct((nd, t), f32)],
        compiler_params=pltpu.CompilerParams(dimension_semantics=("arbitrary",),
                                             vmem_limit_bytes=VMEM_LIMIT),
        name="norm_dt")(h, nw, wdt_t)


def _out_proj_kernel(y_ref, w_ref, h_ref, o_ref):
    o_ref[...] = h_ref[...] + jnp.dot(y_ref[...], w_ref[...], preferred_element_type=f32)


def _out_proj_final_kernel(y_ref, w_ref, h_ref, nw_ref, o_ref):
    acc = h_ref[...] + jnp.dot(y_ref[...], w_ref[...], preferred_element_type=f32)
    ms = jnp.mean(acc * acc, axis=-1, keepdims=True)
    o_ref[...] = acc * lax.rsqrt(ms + EPS) * nw_ref[...]


def _out_proj(y, w, h, final_nw=None, *, tm=1024):
    t, e = y.shape
    d = w.shape[1]
    in_specs = [pl.BlockSpec((tm, e), lambda i: (i, 0)),
                pl.BlockSpec((e, d), lambda i: (0, 0)),
                pl.BlockSpec((tm, d), lambda i: (i, 0))]
    args = [y, w, h]
    body = _out_proj_kernel
    if final_nw is not None:
        in_specs.append(pl.BlockSpec((1, d), lambda i: (0, 0)))
        args.append(final_nw)
        body = _out_proj_final_kernel
    return pl.pallas_call(
        body, grid=(t // tm,), in_specs=in_specs,
        out_specs=pl.BlockSpec((tm, d), lambda i: (i, 0)),
        out_shape=jax.ShapeDtypeStruct((t, d), f32),
        compiler_params=pltpu.CompilerParams(dimension_semantics=("arbitrary",),
                                             vmem_limit_bytes=VMEM_LIMIT),
        name="out_proj")(*args)


def _split3(v):
    hi = v.astype(bf16).astype(f32)
    rem = v - hi
    mid = rem.astype(bf16).astype(f32)
    return hi, mid, rem - mid


_N_VALS = 24
_SPLIT_ROWS = 3 * _N_VALS
_STRIP_BASE = 80
_STRIP_ROWS = 48
_MASKED = -1e30


def _selectors():
    n_dir = 2 * HEADS_PER_GROUP
    sel_g = np.zeros((_STRIP_BASE, n_dir * CHUNK), np.float32)
    for j in range(n_dir):
        for p in range(3):
            sel_g[p * _N_VALS + j, j * CHUNK:(j + 1) * CHUNK] = 1.0 if j < HEADS_PER_GROUP else -1.0
    sels = []
    for fwd_v, bwd_v in ((8, 12), (16, 20)):
        s = np.zeros((CHUNK, 2 * GROUP_WIDTH), np.float32)
        for a, v0 in enumerate((fwd_v, bwd_v)):
            for r in range(HEADS_PER_GROUP):
                lo = a * GROUP_WIDTH + r * SSD_HEAD_DIM
                for p in range(3):
                    s[p * _N_VALS + v0 + r, lo:lo + SSD_HEAD_DIM] = 1.0
        sels.append(s)
    li = np.arange(CHUNK)[:, None]
    si = np.arange(CHUNK)[None, :]
    mask = np.concatenate([np.where(si > li, _MASKED, 0.0)] * HEADS_PER_GROUP
                          + [np.where(si < li, _MASKED, 0.0)] * HEADS_PER_GROUP, axis=1)
    return (jnp.asarray(sel_g, bf16), jnp.asarray(mask, bf16),
            jnp.asarray(sels[0], bf16), jnp.asarray(sels[1], bf16))


def _ssd_kernel(u_ref, wfirst_ref, wnext_ref, dt_ref, dtb_ref, alog_ref,
                cwx_ref, cwb_ref, cwc_ref, cbx_ref, cbb_ref, cbc_ref, dskip_ref, nw_ref,
                selg_ref, mask_ref, sels_ref, selc_ref,
                o_ref, xp_scr, xbc_scr, strip_scr, col_scr, y_scr, s_scr, edge_scr, hin_scr,
                exp_scr, cb_scr, proj_scr, z_scr):
    seq = u_ref.shape[0]
    n_chunks = seq // CHUNK
    width = GROUP_WIDTH + 2 * SSD_STATE
    step = pl.program_id(0) * pl.num_programs(1) + pl.program_id(1)
    cur = step % 2
    nxt = 1 - cur

    @pl.when(step == 0)
    def _():
        for r0 in range(0, seq, PROJ_ROWS):
            proj_scr[0, r0:r0 + PROJ_ROWS, :] = jnp.dot(
                u_ref[r0:r0 + PROJ_ROWS, :], wfirst_ref[...], preferred_element_type=f32).astype(bf16)

    def project_next(q, nt):
        rows = slice(q * PROJ_ROWS, (q + 1) * PROJ_ROWS)
        cols = slice(nt * GROUP_WIDTH, (nt + 1) * GROUP_WIDTH)
        proj_scr[nxt, rows, cols] = jnp.dot(u_ref[rows, :], wnext_ref[:, cols],
                                            preferred_element_type=f32).astype(bf16)

    xp_scr[0:HALO, :] = jnp.zeros((HALO, width), f32)
    xp_scr[seq + HALO:seq + 2 * HALO, :] = jnp.zeros((HALO, width), f32)

    for c in range(n_chunks):
        src_rows = slice(c * CHUNK, (c + 1) * CHUNK)
        rows = slice(c * CHUNK + HALO, (c + 1) * CHUNK + HALO)
        xp_scr[rows, :] = proj_scr[cur, src_rows, GROUP_WIDTH:GROUP_WIDTH + width].astype(f32)
        z_scr[src_rows, :] = proj_scr[cur, src_rows, 0:GROUP_WIDTH]

    n_dir = 2 * HEADS_PER_GROUP
    stack = lambda v: jnp.concatenate([v] * n_chunks, axis=0)
    raw = (jnp.concatenate([dt_ref[:, c * CHUNK:(c + 1) * CHUNK] for c in range(n_chunks)], axis=0)
           + stack(dtb_ref[...]))
    is_fwd = (lax.broadcasted_iota(jnp.int32, raw.shape, 0) & (n_dir - 1)) < HEADS_PER_GROUP
    dt = jnp.maximum(raw, 0.0) + jnp.log1p(jnp.exp(-jnp.abs(raw)))
    a2 = -(dt * stack(jnp.exp(alog_ref[...]) * LOG2E))
    tri = (lax.broadcasted_iota(jnp.int32, (CHUNK, CHUNK), 0)
           <= lax.broadcasted_iota(jnp.int32, (CHUNK, CHUNK), 1)).astype(bf16)
    sums = jnp.dot(jnp.concatenate(_split3(a2), axis=0).astype(bf16), tri, preferred_element_type=f32)
    n_rows = raw.shape[0]
    incl = sums[0:n_rows] + sums[n_rows:2 * n_rows] + sums[2 * n_rows:3 * n_rows]
    tot = incl[:, CHUNK - 1:CHUNK]
    cs = jnp.where(is_fwd, incl, incl - a2)
    log2_dt = jnp.maximum(jnp.log2(dt), -1e30)
    src_parts = _split3(jnp.where(is_fwd, log2_dt - cs, cs + log2_dt))
    to_end = jnp.exp2(tot - cs)
    from_start = jnp.exp2(cs)
    state_w = jnp.where(is_fwd, to_end, from_start) * dt
    carry_w = jnp.where(is_fwd, from_start, to_end)
    val_parts = list(zip(_split3(cs), _split3(state_w), _split3(carry_w)))

    own_block = (lax.broadcasted_iota(jnp.int32, (8, n_dir * CHUNK), 1) // CHUNK
                 == lax.broadcasted_iota(jnp.int32, (8, n_dir * CHUNK), 0))
    eye = (lax.broadcasted_iota(jnp.int32, (CHUNK, CHUNK), 0)
           == lax.broadcasted_iota(jnp.int32, (CHUNK, CHUNK), 1)).astype(bf16)
    ones_rows = jnp.ones((CHUNK - _SPLIT_ROWS, CHUNK), f32)
    zero_rows = jnp.zeros((8, n_dir * CHUNK), f32)
    contract1 = (((1,), (1,)), ((), ()))
    for c in range(n_chunks):
        own = slice(c * n_dir, (c + 1) * n_dir)
        rows = jnp.concatenate([v[own] for part in val_parts for v in part] + [ones_rows],
                               axis=0).astype(bf16)
        col_scr[c * CHUNK:(c + 1) * CHUNK, :] = lax.dot_general(
            eye, rows, contract1, preferred_element_type=f32).astype(bf16)
        strip = []
        for part in src_parts:
            tiled = jnp.concatenate([part[own]] * n_dir, axis=1)
            strip += [jnp.where(own_block, tiled, 0.0), zero_rows]
        strip_scr[c] = jnp.concatenate(strip, axis=0).astype(bf16)

    conv_w = jnp.concatenate([cwx_ref[...], cwb_ref[...], cwc_ref[...]], axis=1)
    conv_b = jnp.concatenate([cbx_ref[...], cbb_ref[...], cbc_ref[...]], axis=1)
    dskip = dskip_ref[...]
    nw = nw_ref[...]
    lane_head = lax.broadcasted_iota(jnp.int32, (CHUNK, GROUP_WIDTH), 1) // SSD_HEAD_DIM
    contract0 = (((0,), (0,)), ((), ()))

    chunks_per_tile = PROJ_ROWS // CHUNK
    n_col_tiles = (GROUP_WIDTH + width) // GROUP_WIDTH
    tile_at = {}
    for q in range(seq // PROJ_ROWS):
        slots = [(q * chunks_per_tile + i, k) for i in range(chunks_per_tile) for k in (1, 3)]
        for nt in range(n_col_tiles):
            tile_at[slots[nt * len(slots) // n_col_tiles]] = (q, nt)

    def local_pass(c):
        base = c * CHUNK
        win = xp_scr[pl.ds(base, CHUNK + 2 * HALO), :]
        acc = jnp.broadcast_to(conv_b, (CHUNK, width))
        for k in range(SSD_CONV):
            acc = acc + _rows_ahead(win, HALO - SSD_CONV // 2 + k) * conv_w[k:k + 1, :]
            if (c, k) in tile_at:
                project_next(*tile_at[(c, k)])
        xbc = _silu_of_twice(acc)
        xbc_scr[pl.ds(base, CHUNK), :] = xbc.astype(bf16)
        xs = xbc[:, 0:GROUP_WIDTH]
        bm = xbc[:, GROUP_WIDTH:GROUP_WIDTH + SSD_STATE].astype(bf16)
        cm = xbc[:, GROUP_WIDTH + SSD_STATE:width].astype(bf16)
        y_scr[pl.ds(base, CHUNK), :] = xs * dskip
        col = col_scr[pl.ds(base, CHUNK), :]
        exp_scr[c] = jnp.dot(col, jnp.concatenate([selg_ref[...], strip_scr[c]], axis=0),
                             preferred_element_type=f32)
        cb_scr[c] = lax.dot_general(cm, bm, contract1, preferred_element_type=f32)
        state_w = jnp.dot(col, sels_ref[...], preferred_element_type=f32)
        weighted = jnp.concatenate([(xs * state_w[:, 0:GROUP_WIDTH]).astype(bf16),
                                    (xs * state_w[:, GROUP_WIDTH:]).astype(bf16)], axis=1)
        s_scr[c] = lax.dot_general(bm, weighted, contract0, preferred_element_type=f32)
        edge = jnp.concatenate([col[CHUNK - 16:CHUNK, :], col[0:16, :]], axis=0)
        edge_scr[c] = jnp.dot(edge, selc_ref[...], preferred_element_type=f32)

    for c in range(n_chunks):
        local_pass(c)

    def recur(i, carry):
        hf, hb = carry
        cf = i
        cr = n_chunks - 1 - i
        hin_scr[cf, :, 0:GROUP_WIDTH] = hf.astype(bf16)
        hin_scr[cr, :, GROUP_WIDTH:2 * GROUP_WIDTH] = hb.astype(bf16)
        hf = edge_scr[cf][15:16, 0:GROUP_WIDTH] * hf + s_scr[cf][:, 0:GROUP_WIDTH]
        hb = edge_scr[cr][16:17, GROUP_WIDTH:2 * GROUP_WIDTH] * hb + s_scr[cr][:, GROUP_WIDTH:2 * GROUP_WIDTH]
        return hf, hb

    h0 = jnp.zeros((SSD_STATE, GROUP_WIDTH), f32)
    lax.fori_loop(0, n_chunks, recur, (h0, h0))


    li = lax.broadcasted_iota(jnp.int32, (CHUNK, CHUNK), 0)
    si = lax.broadcasted_iota(jnp.int32, (CHUNK, CHUNK), 1)
    lower = si <= li
    upper = si >= li

    def output_pass(c):
        base = c * CHUNK
        xbc = xbc_scr[pl.ds(base, CHUNK), :]
        xs = xbc[:, 0:GROUP_WIDTH]
        cm = xbc[:, GROUP_WIDTH + SSD_STATE:width]
        col = col_scr[pl.ds(base, CHUNK), :]
        carried = (jnp.dot(col, selc_ref[...], preferred_element_type=f32)
                   * jnp.dot(cm, hin_scr[c], preferred_element_type=f32))
        cb = cb_scr[c]
        g_heads, x_heads = [], []
        for r in range(HEADS_PER_GROUP):
            arg_f = exp_scr[c, :, r * CHUNK:(r + 1) * CHUNK]
            arg_b = exp_scr[c, :, (HEADS_PER_GROUP + r) * CHUNK:(HEADS_PER_GROUP + r + 1) * CHUNK]
            decay = (jnp.exp2(jnp.where(lower, arg_f, _MASKED))
                     + jnp.exp2(jnp.where(upper, arg_b, _MASKED)))
            g_heads.append((cb * decay).astype(bf16))
            x_heads.append(jnp.where(lane_head == r, xs, jnp.zeros_like(xs)))
        y = y_scr[pl.ds(base, CHUNK), :] + jnp.dot(
            jnp.concatenate(g_heads, axis=1), jnp.concatenate(x_heads, axis=0),
            preferred_element_type=f32)
        y = y + carried[:, 0:GROUP_WIDTH] + carried[:, GROUP_WIDTH:]
        y = y * _silu_of_twice(z_scr[pl.ds(base, CHUNK), :].astype(f32))
        ms = jnp.mean(y * y, axis=-1, keepdims=True)
        o_ref[pl.ds(base, CHUNK), :] = (y * lax.rsqrt(ms + EPS) * nw).astype(o_ref.dtype)

    for c in range(n_chunks):
        output_pass(c)


def _ssd_mixer(u, w_groups, dt_t, dt_bias, a_log, conv_w, conv_b, dskip, norm_w):
    bsz, seq, d = u.shape
    e = SSD_GROUPS * GROUP_WIDTH
    width = GROUP_WIDTH + 2 * SSD_STATE
    n_proj = GROUP_WIDTH + width
    next_b = lambda b, g: jnp.minimum(b + (g + 1) // SSD_GROUPS, bsz - 1)
    in_specs = [
        pl.BlockSpec((None, seq, d), lambda b, g: (next_b(b, g), 0, 0)),
        pl.BlockSpec((None, d, n_proj), lambda b, g: (0, 0, 0)),
        pl.BlockSpec((None, d, n_proj), lambda b, g: ((g + 1) % SSD_GROUPS, 0, 0)),
        pl.BlockSpec((8, seq), lambda b, g: (g, b)),
        pl.BlockSpec((8, 1), lambda b, g: (g, 0)),
        pl.BlockSpec((8, 1), lambda b, g: (g, 0)),
        pl.BlockSpec((SSD_CONV, GROUP_WIDTH), lambda b, g: (0, g)),
        pl.BlockSpec((SSD_CONV, SSD_STATE), lambda b, g: (0, e // SSD_STATE + g)),
        pl.BlockSpec((SSD_CONV, SSD_STATE), lambda b, g: (0, e // SSD_STATE + SSD_GROUPS + g)),
        pl.BlockSpec((1, GROUP_WIDTH), lambda b, g: (0, g)),
        pl.BlockSpec((1, SSD_STATE), lambda b, g: (0, e // SSD_STATE + g)),
        pl.BlockSpec((1, SSD_STATE), lambda b, g: (0, e // SSD_STATE + SSD_GROUPS + g)),
        pl.BlockSpec((1, GROUP_WIDTH), lambda b, g: (0, g)),
        pl.BlockSpec((1, GROUP_WIDTH), lambda b, g: (0, g)),
    ]
    selectors = _selectors()
    in_specs += [pl.BlockSpec(s.shape, lambda b, g: (0, 0)) for s in selectors]
    return pl.pallas_call(
        _ssd_kernel, grid=(bsz, SSD_GROUPS), in_specs=in_specs,
        out_specs=pl.BlockSpec((None, seq, GROUP_WIDTH), lambda b, g: (b, 0, g)),
        out_shape=jax.ShapeDtypeStruct((bsz, seq, e), bf16),
        scratch_shapes=[
            pltpu.VMEM((seq + 2 * HALO, width), f32),
            pltpu.VMEM((seq, width), bf16),
            pltpu.VMEM((seq // CHUNK, _STRIP_ROWS, 2 * HEADS_PER_GROUP * CHUNK), bf16),
            pltpu.VMEM((seq, CHUNK), bf16),
            pltpu.VMEM((seq, GROUP_WIDTH), f32),
            pltpu.VMEM((seq // CHUNK, SSD_STATE, 2 * GROUP_WIDTH), f32),
            pltpu.VMEM((seq // CHUNK, 32, 2 * GROUP_WIDTH), f32),
            pltpu.VMEM((seq // CHUNK, SSD_STATE, 2 * GROUP_WIDTH), bf16),
            pltpu.VMEM((seq // CHUNK, CHUNK, 2 * HEADS_PER_GROUP * CHUNK), f32),
            pltpu.VMEM((seq // CHUNK, CHUNK, CHUNK), f32),
            pltpu.VMEM((2, seq, n_proj), bf16),
            pltpu.VMEM((seq, GROUP_WIDTH), bf16),
        ],
        compiler_params=pltpu.CompilerParams(dimension_semantics=("arbitrary", "arbitrary"),
                                             vmem_limit_bytes=SSD_VMEM_LIMIT),
        name="ssd_mixer")(u, w_groups, w_groups, dt_t, dt_bias, a_log,
                          conv_w, conv_w, conv_w, conv_b, conv_b, conv_b, dskip, norm_w,
                          *selectors)


def _window_sum(win, w):
    half = w // 2
    if w == 2:
        return _rows_ahead(win, HALO - 1) + _rows_ahead(win, HALO)
    t = win + pltpu.roll(win, win.shape[0] - 1, axis=0)
    span = 2
    while span * 2 < w:
        t = t + pltpu.roll(t, win.shape[0] - span, axis=0)
        span *= 2
    return _rows_ahead(t, HALO - half) + _rows_ahead(t, HALO)


def _pool_kernel(v_ref, gate_ref, mix_ref, scale_ref, o_ref, vp_scr):
    seq, gd = v_ref.shape
    n_steps = seq // POOL_ROWS
    gi = pl.program_id(1)

    vp_scr[0:HALO, :] = jnp.zeros((HALO, gd), f32)
    vp_scr[seq + HALO:seq + 2 * HALO, :] = jnp.zeros((HALO, gd), f32)

    def fill(c, carry):
        base = pl.multiple_of(c * POOL_ROWS, POOL_ROWS)
        vp_scr[pl.ds(base + HALO, POOL_ROWS), :] = v_ref[pl.ds(base, POOL_ROWS), :].astype(f32)
        return carry

    lax.fori_loop(0, n_steps, fill, 0)
    scale = scale_ref[...]

    for k, w in enumerate(POOL_WINDOWS):
        @pl.when(gi == k)
        def _(w=w):
            half = w // 2

            def step(c, carry):
                base = pl.multiple_of(c * POOL_ROWS, POOL_ROWS)
                win = vp_scr[pl.ds(base, POOL_ROWS + 2 * HALO), :]
                pos = base + lax.broadcasted_iota(jnp.int32, (POOL_ROWS, 1), 0)
                cnt = (jnp.minimum(pos + half, seq) - jnp.maximum(pos - half, 0)).astype(f32)
                pooled = _window_sum(win, w) / cnt - win[HALO:HALO + POOL_ROWS, :]
                mixed = jnp.dot(pooled.astype(bf16), mix_ref[...], preferred_element_type=f32)
                gate = gate_ref[pl.ds(base, POOL_ROWS), :].astype(f32)
                o_ref[pl.ds(base, POOL_ROWS), :] = (mixed * scale * _silu_of_twice(gate)).astype(o_ref.dtype)
                return carry

            lax.fori_loop(0, n_steps, step, 0)


def _pool_core(proj, mix_w, scale):
    bsz, seq, e2 = proj.shape
    e = e2 // 2
    ng = len(POOL_WINDOWS)
    gd = e // ng
    return pl.pallas_call(
        _pool_kernel, grid=(bsz, ng),
        in_specs=[pl.BlockSpec((None, seq, gd), lambda b, g: (b, 0, g)),
                  pl.BlockSpec((None, seq, gd), lambda b, g: (b, 0, ng + g)),
                  pl.BlockSpec((None, gd, gd), lambda b, g: (g, 0, 0)),
                  pl.BlockSpec((1, gd), lambda b, g: (0, g))],
        out_specs=pl.BlockSpec((None, seq, gd), lambda b, g: (b, 0, g)),
        out_shape=jax.ShapeDtypeStruct((bsz, seq, e), bf16),
        scratch_shapes=[pltpu.VMEM((seq + 2 * HALO, gd), f32)],
        compiler_params=pltpu.CompilerParams(dimension_semantics=("arbitrary", "arbitrary"),
                                             vmem_limit_bytes=VMEM_LIMIT),
        name="pool_core")(proj, proj, mix_w, scale)


def _group_major(p):
    return p.reshape(2, SSD_GROUPS, HEADS_PER_GROUP).transpose(1, 0, 2).reshape(-1, 1)


def kernel(x, norm_w, ssd_w_in, ssd_conv_w, ssd_conv_b, ssd_dt_bias, ssd_a_log, ssd_d, ssd_norm_w, ssd_w_out, pool_w_in, pool_mix_w, pool_scale, pool_w_out, final_norm_w):
    bsz, seq, d = x.shape
    t = bsz * seq
    depth = norm_w.shape[0]
    e = ssd_w_out.shape[1]
    n_main = 2 * e + 2 * SSD_GROUPS * SSD_STATE
    h = x.reshape(t, d)
    for i in range(depth):
        j = i // 2
        nw = norm_w[i].reshape(1, d)
        final_nw = final_norm_w.reshape(1, d) if i == depth - 1 else None
        if i % 2 == 0:
            w = ssd_w_in[j]
            gs = SSD_GROUPS * SSD_STATE
            per_group = lambda cols, width: cols.reshape(d, SSD_GROUPS, width)
            w_groups = jnp.concatenate(
                [per_group(0.5 * w[:, :e], GROUP_WIDTH), per_group(w[:, e:2 * e], GROUP_WIDTH),
                 per_group(w[:, 2 * e:2 * e + gs], SSD_STATE), per_group(w[:, 2 * e + gs:n_main], SSD_STATE)],
                axis=2).transpose(1, 0, 2).astype(bf16)
            w_dt = w[:, n_main:]
            wdt_t = (w_dt.reshape(d, 2, SSD_GROUPS, HEADS_PER_GROUP).transpose(2, 1, 3, 0)
                     .reshape(-1, d).astype(bf16))
            u, dt_t = _norm_dt(h, nw, wdt_t)
            y = _ssd_mixer(u.reshape(bsz, seq, d), w_groups, dt_t,
                           _group_major(ssd_dt_bias[j]), _group_major(ssd_a_log[j]),
                           0.5 * ssd_conv_w[j], 0.5 * ssd_conv_b[j].reshape(1, -1),
                           jnp.repeat(ssd_d[j], SSD_HEAD_DIM).reshape(1, e),
                           ssd_norm_w[j].reshape(1, e))
            h = _out_proj(y.reshape(t, e), ssd_w_out[j].astype(bf16), h, final_nw)
        else:
            gate_half = jnp.where(jnp.arange(2 * e) < e, 1.0, 0.5).astype(f32)
            proj = _in_proj(h, nw, (pool_w_in[j] * gate_half).astype(bf16))
            y = _pool_core(proj.reshape(bsz, seq, 2 * e), pool_mix_w[j].astype(bf16),
                           pool_scale[j].reshape(1, e))
            h = _out_proj(y.reshape(t, e), pool_w_out[j].astype(bf16), h, final_nw)
    return h.reshape(bsz, seq, d)
```

```python
import functools

import jax
import jax.numpy as jnp
import numpy as np
from jax import lax
from jax.experimental import pallas as pl
from jax.experimental.pallas import tpu as pltpu

f32 = jnp.float32
bf16 = jnp.bfloat16

EPS = 1e-6
LANES = 128
LOG2E = 1.4426950408889634
SSD_HEAD_DIM = 64
SSD_GROUPS = 8
SSD_STATE = 128
SSD_CONV = 5
HEADS_PER_GROUP = 4
GROUP_WIDTH = HEADS_PER_GROUP * SSD_HEAD_DIM
CHUNK = 128
HALO = 8
POOL_WINDOWS = (2, 4, 8, 16)
POOL_ROWS = 256
PROJ_ROWS = 1024

VMEM_LIMIT = 48 * 1024 * 1024
SSD_VMEM_LIMIT = 58 * 1024 * 1024


def _silu_of_twice(half):
    return half + half * jnp.tanh(half)


def _rows_ahead(win, k):
    n = win.shape[0]
    if k % 8 == 0:
        return win[k:k + n - 2 * HALO, :]
    return pltpu.roll(win, n - k, axis=0)[0:n - 2 * HALO, :]


def _normed(h_ref, nw_ref):
    xv = h_ref[...]
    ms = jnp.mean(xv * xv, axis=-1, keepdims=True)
    return (xv * lax.rsqrt(ms + EPS) * nw_ref[...]).astype(bf16)


def _in_proj_kernel(h_ref, nw_ref, w_ref, o_ref, u_scr):
    @pl.when(pl.program_id(1) == 0)
    def _():
        u_scr[...] = _normed(h_ref, nw_ref)

    o_ref[...] = jnp.dot(u_scr[...], w_ref[...], preferred_element_type=f32).astype(o_ref.dtype)


def _in_proj(h, nw, w, *, tm=1024, tn=1024):
    t, d = h.shape
    n = w.shape[1]
    return pl.pallas_call(
        _in_proj_kernel, grid=(t // tm, n // tn),
        in_specs=[pl.BlockSpec((tm, d), lambda i, j: (i, 0)),
                  pl.BlockSpec((1, d), lambda i, j: (0, 0)),
                  pl.BlockSpec((d, tn), lambda i, j: (0, j))],
        out_specs=pl.BlockSpec((tm, tn), lambda i, j: (i, j)),
        out_shape=jax.ShapeDtypeStruct((t, n), bf16),
        scratch_shapes=[pltpu.VMEM((tm, d), bf16)],
        compiler_params=pltpu.CompilerParams(dimension_semantics=("arbitrary", "arbitrary"),
                                             vmem_limit_bytes=VMEM_LIMIT),
        name="in_proj")(h, nw, w)


def _norm_dt_kernel(h_ref, nw_ref, wdt_ref, u_ref, dt_ref):
    u = _normed(h_ref, nw_ref)
    u_ref[...] = u
    dt_ref[...] = lax.dot_general(wdt_ref[...], u, (((1,), (1,)), ((), ())),
                                  preferred_element_type=f32)


def _norm_dt(h, nw, wdt_t, *, tm=1024):
    t, d = h.shape
    nd = wdt_t.shape[0]
    return pl.pallas_call(
        _norm_dt_kernel, grid=(t // tm,),
        in_specs=[pl.BlockSpec((tm, d), lambda i: (i, 0)),
                  pl.BlockSpec((1, d), lambda i: (0, 0)),
                  pl.BlockSpec((nd, d), lambda i: (0, 0))],
        out_specs=[pl.BlockSpec((tm, d), lambda i: (i, 0)), pl.BlockSpec((nd, tm), lambda i: (0, i))],
        out_shape=[jax.ShapeDtypeStruct((t, d), bf16), jax.ShapeDtypeStruct((nd, t), f32)],
        compiler_params=pltpu.CompilerParams(dimension_semantics=("arbitrary",),
                                             vmem_limit_bytes=VMEM_LIMIT),
        name="norm_dt")(h, nw, wdt_t)


def _out_proj_kernel(y_ref, w_ref, h_ref, o_ref):
    o_ref[...] = h_ref[...] + jnp.dot(y_ref[...], w_ref[...], preferred_element_type=f32)


def _out_proj_final_kernel(y_ref, w_ref, h_ref, nw_ref, o_ref):
    acc = h_ref[...] + jnp.dot(y_ref[...], w_ref[...], preferred_element_type=f32)
    ms = jnp.mean(acc * acc, axis=-1, keepdims=True)
    o_ref[...] = acc * lax.rsqrt(ms + EPS) * nw_ref[...]


def _out_proj(y, w, h, final_nw=None, *, tm=1024):
    t, e = y.shape
    d = w.shape[1]
    in_specs = [pl.BlockSpec((tm, e), lambda i: (i, 0)),
                pl.BlockSpec((e, d), lambda i: (0, 0)),
                pl.BlockSpec((tm, d), lambda i: (i, 0))]
    args = [y, w, h]
    body = _out_proj_kernel
    if final_nw is not None:
        in_specs.append(pl.BlockSpec((1, d), lambda i: (0, 0)))
        args.append(final_nw)
        body = _out_proj_final_kernel
    return pl.pallas_call(
        body, grid=(t // tm,), in_specs=in_specs,
        out_specs=pl.BlockSpec((tm, d), lambda i: (i, 0)),
        out_shape=jax.ShapeDtypeStruct((t, d), f32),
        compiler_params=pltpu.CompilerParams(dimension_semantics=("arbitrary",),
                                             vmem_limit_bytes=VMEM_LIMIT),
        name="out_proj")(*args)


def _split3(v):
    hi = v.astype(bf16).astype(f32)
    rem = v - hi
    mid = rem.astype(bf16).astype(f32)
    return hi, mid, rem - mid


_N_VALS = 24
_SPLIT_ROWS = 3 * _N_VALS
_STRIP_BASE = 80
_STRIP_ROWS = 48
_MASKED = -1e30


def _selectors():
    n_dir = 2 * HEADS_PER_GROUP
    sel_g = np.zeros((_STRIP_BASE, n_dir * CHUNK), np.float32)
    for j in range(n_dir):
        for p in range(3):
            sel_g[p * _N_VALS + j, j * CHUNK:(j + 1) * CHUNK] = 1.0 if j < HEADS_PER_GROUP else -1.0
    sels = []
    for fwd_v, bwd_v in ((8, 12), (16, 20)):
        s = np.zeros((CHUNK, 2 * GROUP_WIDTH), np.float32)
        for a, v0 in enumerate((fwd_v, bwd_v)):
            for r in range(HEADS_PER_GROUP):
                lo = a * GROUP_WIDTH + r * SSD_HEAD_DIM
                for p in range(3):
                    s[p * _N_VALS + v0 + r, lo:lo + SSD_HEAD_DIM] = 1.0
        sels.append(s)
    li = np.arange(CHUNK)[:, None]
    si = np.arange(CHUNK)[None, :]
    mask = np.concatenate([np.where(si > li, _MASKED, 0.0)] * HEADS_PER_GROUP
                          + [np.where(si < li, _MASKED, 0.0)] * HEADS_PER_GROUP, axis=1)
    return (jnp.asarray(sel_g, bf16), jnp.asarray(mask, bf16),
            jnp.asarray(sels[0], bf16), jnp.asarray(sels[1], bf16))


def _ssd_kernel(u_ref, wfirst_ref, wnext_ref, dt_ref, dtb_ref, alog_ref,
                cwx_ref, cwb_ref, cwc_ref, cbx_ref, cbb_ref, cbc_ref, dskip_ref, nw_ref,
                selg_ref, mask_ref, sels_ref, selc_ref,
                o_ref, xp_scr, xbc_scr, strip_scr, col_scr, y_scr, s_scr, edge_scr, hin_scr,
                exp_scr, cb_scr, proj_scr, z_scr):
    seq = u_ref.shape[0]
    n_chunks = seq // CHUNK
    width = GROUP_WIDTH + 2 * SSD_STATE
    step = pl.program_id(0) * pl.num_programs(1) + pl.program_id(1)
    cur = step % 2
    nxt = 1 - cur

    @pl.when(step == 0)
    def _():
        for r0 in range(0, seq, PROJ_ROWS):
            proj_scr[0, r0:r0 + PROJ_ROWS, :] = jnp.dot(
                u_ref[r0:r0 + PROJ_ROWS, :], wfirst_ref[...], preferred_element_type=f32).astype(bf16)

    def project_next(q, nt):
        rows = slice(q * PROJ_ROWS, (q + 1) * PROJ_ROWS)
        cols = slice(nt * GROUP_WIDTH, (nt + 1) * GROUP_WIDTH)
        proj_scr[nxt, rows, cols] = jnp.dot(u_ref[rows, :], wnext_ref[:, cols],
                                            preferred_element_type=f32).astype(bf16)

    xp_scr[0:HALO, :] = jnp.zeros((HALO, width), f32)
    xp_scr[seq + HALO:seq + 2 * HALO, :] = jnp.zeros((HALO, width), f32)

    for c in range(n_chunks):
        src_rows = slice(c * CHUNK, (c + 1) * CHUNK)
        rows = slice(c * CHUNK + HALO, (c + 1) * CHUNK + HALO)
        xp_scr[rows, :] = proj_scr[cur, src_rows, GROUP_WIDTH:GROUP_WIDTH + width].astype(f32)
        z_scr[src_rows, :] = proj_scr[cur, src_rows, 0:GROUP_WIDTH]

    n_dir = 2 * HEADS_PER_GROUP
    stack = lambda v: jnp.concatenate([v] * n_chunks, axis=0)
    raw = (jnp.concatenate([dt_ref[:, c * CHUNK:(c + 1) * CHUNK] for c in range(n_chunks)], axis=0)
           + stack(dtb_ref[...]))
    is_fwd = (lax.broadcasted_iota(jnp.int32, raw.shape, 0) & (n_dir - 1)) < HEADS_PER_GROUP
    dt = jnp.maximum(raw, 0.0) + jnp.log1p(jnp.exp(-jnp.abs(raw)))
    a2 = -(dt * stack(jnp.exp(alog_ref[...]) * LOG2E))
    tri = (lax.broadcasted_iota(jnp.int32, (CHUNK, CHUNK), 0)
           <= lax.broadcasted_iota(jnp.int32, (CHUNK, CHUNK), 1)).astype(bf16)
    sums = jnp.dot(jnp.concatenate(_split3(a2), axis=0).astype(bf16), tri, preferred_element_type=f32)
    n_rows = raw.shape[0]
    incl = sums[0:n_rows] + sums[n_rows:2 * n_rows] + sums[2 * n_rows:3 * n_rows]
    tot = incl[:, CHUNK - 1:CHUNK]
    cs = jnp.where(is_fwd, incl, incl - a2)
    log2_dt = jnp.maximum(jnp.log2(dt), -1e30)
    src_parts = _split3(jnp.where(is_fwd, log2_dt - cs, cs + log2_dt))
    to_end = jnp.exp2(tot - cs)
    from_start = jnp.exp2(cs)
    state_w = jnp.where(is_fwd, to_end, from_start) * dt
    carry_w = jnp.where(is_fwd, from_start, to_end)
    val_parts = list(zip(_split3(cs), _split3(state_w), _split3(carry_w)))

    own_block = (lax.broadcasted_iota(jnp.int32, (8, n_dir * CHUNK), 1) // CHUNK
                 == lax.broadcasted_iota(jnp.int32, (8, n_dir * CHUNK), 0))
    eye = (lax.broadcasted_iota(jnp.int32, (CHUNK, CHUNK), 0)
           == lax.broadcasted_iota(jnp.int32, (CHUNK, CHUNK), 1)).astype(bf16)
    ones_rows = jnp.ones((CHUNK - _SPLIT_ROWS, CHUNK), f32)
    zero_rows = jnp.zeros((8, n_dir * CHUNK), f32)
    contract1 = (((1,), (1,)), ((), ()))
    for c in range(n_chunks):
        own = slice(c * n_dir, (c + 1) * n_dir)
        rows = jnp.concatenate([v[own] for part in val_parts for v in part] + [ones_rows],
                               axis=0).astype(bf16)
        col_scr[c * CHUNK:(c + 1) * CHUNK, :] = lax.dot_general(
            eye, rows, contract1, preferred_element_type=f32).astype(bf16)
        strip = []
        for part in src_parts:
            tiled = jnp.concatenate([part[own]] * n_dir, axis=1)
            strip += [jnp.where(own_block, tiled, 0.0), zero_rows]
        strip_scr[c] = jnp.concatenate(strip, axis=0).astype(bf16)

    conv_w = jnp.concatenate([cwx_ref[...], cwb_ref[...], cwc_ref[...]], axis=1)
    conv_b = jnp.concatenate([cbx_ref[...], cbb_ref[...], cbc_ref[...]], axis=1)
    dskip = dskip_ref[...]
    nw = nw_ref[...]
    lane_head = lax.broadcasted_iota(jnp.int32, (CHUNK, GROUP_WIDTH), 1) // SSD_HEAD_DIM
    contract0 = (((0,), (0,)), ((), ()))

    chunks_per_tile = PROJ_ROWS // CHUNK
    n_col_tiles = (GROUP_WIDTH + width) // GROUP_WIDTH
    tile_at = {}
    for q in range(seq // PROJ_ROWS):
        slots = [(q * chunks_per_tile + i, k) for i in range(chunks_per_tile) for k in (1, 3)]
        for nt in range(n_col_tiles):
            tile_at[slots[nt * len(slots) // n_col_tiles]] = (q, nt)

    def local_pass(c):
        base = c * CHUNK
        win = xp_scr[pl.ds(base, CHUNK + 2 * HALO), :]
        acc = jnp.broadcast_to(conv_b, (CHUNK, width))
        for k in range(SSD_CONV):
            acc = acc + _rows_ahead(win, HALO - SSD_CONV // 2 + k) * conv_w[k:k + 1, :]
            if (c, k) in tile_at:
                project_next(*tile_at[(c, k)])
        xbc = _silu_of_twice(acc)
        xbc_scr[pl.ds(base, CHUNK), :] = xbc.astype(bf16)
        xs = xbc[:, 0:GROUP_WIDTH]
        bm = xbc[:, GROUP_WIDTH:GROUP_WIDTH + SSD_STATE].astype(bf16)
        cm = xbc[:, GROUP_WIDTH + SSD_STATE:width].astype(bf16)
        y_scr[pl.ds(base, CHUNK), :] = xs * dskip
        col = col_scr[pl.ds(base, CHUNK), :]
        exp_scr[c] = jnp.dot(col, jnp.concatenate([selg_ref[...], strip_scr[c]], axis=0),
                             preferred_element_type=f32)
        cb_scr[c] = lax.dot_general(cm, bm, contract1, preferred_element_type=f32)
        state_w = jnp.dot(col, sels_ref[...], preferred_element_type=f32)
        weighted = jnp.concatenate([(xs * state_w[:, 0:GROUP_WIDTH]).astype(bf16),
                                    (xs * state_w[:, GROUP_WIDTH:]).astype(bf16)], axis=1)
        s_scr[c] = lax.dot_general(bm, weighted, contract0, preferred_element_type=f32)
        edge = jnp.concatenate([col[CHUNK - 16:CHUNK, :], col[0:16, :]], axis=0)
        edge_scr[c] = jnp.dot(edge, selc_ref[...], preferred_element_type=f32)

    for c in range(n_chunks):
        local_pass(c)

    def recur(i, carry):
        hf, hb = carry
        cf = i
        cr = n_chunks - 1 - i
        hin_scr[cf, :, 0:GROUP_WIDTH] = hf.astype(bf16)
        hin_scr[cr, :, GROUP_WIDTH:2 * GROUP_WIDTH] = hb.astype(bf16)
        hf = edge_scr[cf][15:16, 0:GROUP_WIDTH] * hf + s_scr[cf][:, 0:GROUP_WIDTH]
        hb = edge_scr[cr][16:17, GROUP_WIDTH:2 * GROUP_WIDTH] * hb + s_scr[cr][:, GROUP_WIDTH:2 * GROUP_WIDTH]
        return hf, hb

    h0 = jnp.zeros((SSD_STATE, GROUP_WIDTH), f32)
    lax.fori_loop(0, n_chunks, recur, (h0, h0))


    li = lax.broadcasted_iota(jnp.int32, (CHUNK, CHUNK), 0)
    si = lax.broadcasted_iota(jnp.int32, (CHUNK, CHUNK), 1)
    lower = si <= li
    upper = si >= li

    def output_pass(c):
        base = c * CHUNK
        xbc = xbc_scr[pl.ds(base, CHUNK), :]
        xs = xbc[:, 0:GROUP_WIDTH]
        cm = xbc[:, GROUP_WIDTH + SSD_STATE:width]
        col = col_scr[pl.ds(base, CHUNK), :]
        carried = (jnp.dot(col, selc_ref[...], preferred_element_type=f32)
                   * jnp.dot(cm, hin_scr[c], preferred_element_type=f32))
        cb = cb_scr[c]
        g_heads, x_heads = [], []
        for r in range(HEADS_PER_GROUP):
            arg_f = exp_scr[c, :, r * CHUNK:(r + 1) * CHUNK]
            arg_b = exp_scr[c, :, (HEADS_PER_GROUP + r) * CHUNK:(HEADS_PER_GROUP + r + 1) * CHUNK]
            decay = (jnp.exp2(jnp.where(lower, arg_f, _MASKED))
                     + jnp.exp2(jnp.where(upper, arg_b, _MASKED)))
            g_heads.append((cb * decay).astype(bf16))
            x_heads.append(jnp.where(lane_head == r, xs, jnp.zeros_like(xs)))
        y = y_scr[pl.ds(base, CHUNK), :] + jnp.dot(
            jnp.concatenate(g_heads, axis=1), jnp.concatenate(x_heads, axis=0),
            preferred_element_type=f32)
        y = y + carried[:, 0:GROUP_WIDTH] + carried[:, GROUP_WIDTH:]
        y = y * _silu_of_twice(z_scr[pl.ds(base, CHUNK), :].astype(f32))
        ms = jnp.mean(y * y, axis=-1, keepdims=True)
        o_ref[pl.ds(base, CHUNK), :] = (y * lax.rsqrt(ms + EPS) * nw).astype(o_ref.dtype)

    for c in range(n_chunks):
        output_pass(c)


def _ssd_mixer(u, w_groups, dt_t, dt_bias, a_log, conv_w, conv_b, dskip, norm_w):
    bsz, seq, d = u.shape
    e = SSD_GROUPS * GROUP_WIDTH
    width = GROUP_WIDTH + 2 * SSD_STATE
    n_proj = GROUP_WIDTH + width
    next_b = lambda b, g: jnp.minimum(b + (g + 1) // SSD_GROUPS, bsz - 1)
    in_specs = [
        pl.BlockSpec((None, seq, d), lambda b, g: (next_b(b, g), 0, 0)),
        pl.BlockSpec((None, d, n_proj), lambda b, g: (0, 0, 0)),
        pl.BlockSpec((None, d, n_proj), lambda b, g: ((g + 1) % SSD_GROUPS, 0, 0)),
        pl.BlockSpec((8, seq), lambda b, g: (g, b)),
        pl.BlockSpec((8, 1), lambda b, g: (g, 0)),
        pl.BlockSpec((8, 1), lambda b, g: (g, 0)),
        pl.BlockSpec((SSD_CONV, GROUP_WIDTH), lambda b, g: (0, g)),
        pl.BlockSpec((SSD_CONV, SSD_STATE), lambda b, g: (0, e // SSD_STATE + g)),
        pl.BlockSpec((SSD_CONV, SSD_STATE), lambda b, g: (0, e // SSD_STATE + SSD_GROUPS + g)),
        pl.BlockSpec((1, GROUP_WIDTH), lambda b, g: (0, g)),
        pl.BlockSpec((1, SSD_STATE), lambda b, g: (0, e // SSD_STATE + g)),
        pl.BlockSpec((1, SSD_STATE), lambda b, g: (0, e // SSD_STATE + SSD_GROUPS + g)),
        pl.BlockSpec((1, GROUP_WIDTH), lambda b, g: (0, g)),
        pl.BlockSpec((1, GROUP_WIDTH), lambda b, g: (0, g)),
    ]
    selectors = _selectors()
    in_specs += [pl.BlockSpec(s.shape, lambda b, g: (0, 0)) for s in selectors]
    return pl.pallas_call(
        _ssd_kernel, grid=(bsz, SSD_GROUPS), in_specs=in_specs,
        out_specs=pl.BlockSpec((None, seq, GROUP_WIDTH), lambda b, g: (b, 0, g)),
        out_shape=jax.ShapeDtypeStruct((bsz, seq, e), bf16),
        scratch_shapes=[
            pltpu.VMEM((seq + 2 * HALO, width), f32),
            pltpu.VMEM((seq, width), bf16),
            pltpu.VMEM((seq // CHUNK, _STRIP_ROWS, 2 * HEADS_PER_GROUP * CHUNK), bf16),
            pltpu.VMEM((seq, CHUNK), bf16),
            pltpu.VMEM((seq, GROUP_WIDTH), f32),
            pltpu.VMEM((seq // CHUNK, SSD_STATE, 2 * GROUP_WIDTH), f32),
            pltpu.VMEM((seq // CHUNK, 32, 2 * GROUP_WIDTH), f32),
            pltpu.VMEM((seq // CHUNK, SSD_STATE, 2 * GROUP_WIDTH), bf16),
            pltpu.VMEM((seq // CHUNK, CHUNK, 2 * HEADS_PER_GROUP * CHUNK), f32),
            pltpu.VMEM((seq // CHUNK, CHUNK, CHUNK), f32),
            pltpu.VMEM((2, seq, n_proj), bf16),
            pltpu.VMEM((seq, GROUP_WIDTH), bf16),
        ],
        compiler_params=pltpu.CompilerParams(dimension_semantics=("arbitrary", "arbitrary"),
                                             vmem_limit_bytes=SSD_VMEM_LIMIT),
        name="ssd_mixer")(u, w_groups, w_groups, dt_t, dt_bias, a_log,
                          conv_w, conv_w, conv_w, conv_b, conv_b, conv_b, dskip, norm_w,
                          *selectors)


def _window_sum(win, w):
    half = w // 2
    if w == 2:
        return _rows_ahead(win, HALO - 1) + _rows_ahead(win, HALO)
    t = win + pltpu.roll(win, win.shape[0] - 1, axis=0)
    span = 2
    while span * 2 < w:
        t = t + pltpu.roll(t, win.shape[0] - span, axis=0)
        span *= 2
    return _rows_ahead(t, HALO - half) + _rows_ahead(t, HALO)


def _pool_kernel(v_ref, gate_ref, mix_ref, scale_ref, o_ref, vp_scr):
    seq, gd = v_ref.shape
    n_steps = seq // POOL_ROWS
    gi = pl.program_id(1)

    vp_scr[0:HALO, :] = jnp.zeros((HALO, gd), f32)
    vp_scr[seq + HALO:seq + 2 * HALO, :] = jnp.zeros((HALO, gd), f32)

    def fill(c, carry):
        base = pl.multiple_of(c * POOL_ROWS, POOL_ROWS)
        vp_scr[pl.ds(base + HALO, POOL_ROWS), :] = v_ref[pl.ds(base, POOL_ROWS), :].astype(f32)
        return carry

    lax.fori_loop(0, n_steps, fill, 0)
    scale = scale_ref[...]

    for k, w in enumerate(POOL_WINDOWS):
        @pl.when(gi == k)
        def _(w=w):
            half = w // 2

            def step(c, carry):
                base = pl.multiple_of(c * POOL_ROWS, POOL_ROWS)
                win = vp_scr[pl.ds(base, POOL_ROWS + 2 * HALO), :]
                pos = base + lax.broadcasted_iota(jnp.int32, (POOL_ROWS, 1), 0)
                cnt = (jnp.minimum(pos + half, seq) - jnp.maximum(pos - half, 0)).astype(f32)
                pooled = _window_sum(win, w) / cnt - win[HALO:HALO + POOL_ROWS, :]
                mixed = jnp.dot(pooled.astype(bf16), mix_ref[...], preferred_element_type=f32)
                gate = gate_ref[pl.ds(base, POOL_ROWS), :].astype(f32)
                o_ref[pl.ds(base, POOL_ROWS), :] = (mixed * scale * _silu_of_twice(gate)).astype(o_ref.dtype)
                return carry

            lax.fori_loop(0, n_steps, step, 0)


def _pool_core(proj, mix_w, scale):
    bsz, seq, e2 = proj.shape
    e = e2 // 2
    ng = len(POOL_WINDOWS)
    gd = e // ng
    return pl.pallas_call(
        _pool_kernel, grid=(bsz, ng),
        in_specs=[pl.BlockSpec((None, seq, gd), lambda b, g: (b, 0, g)),
                  pl.BlockSpec((None, seq, gd), lambda b, g: (b, 0, ng + g)),
                  pl.BlockSpec((None, gd, gd), lambda b, g: (g, 0, 0)),
                  pl.BlockSpec((1, gd), lambda b, g: (0, g))],
        out_specs=pl.BlockSpec((None, seq, gd), lambda b, g: (b, 0, g)),
        out_shape=jax.ShapeDtypeStruct((bsz, seq, e), bf16),
        scratch_shapes=[pltpu.VMEM((seq + 2 * HALO, gd), f32)],
        compiler_params=pltpu.CompilerParams(dimension_semantics=("arbitrary", "arbitrary"),
                                             vmem_limit_bytes=VMEM_LIMIT),
        name="pool_core")(proj, proj, mix_w, scale)


def _group_major(p):
    return p.reshape(2, SSD_GROUPS, HEADS_PER_GROUP).transpose(1, 0, 2).reshape(-1, 1)


def kernel(x, norm_w, ssd_w_in, ssd_conv_w, ssd_conv_b, ssd_dt_bias, ssd_a_log, ssd_d, ssd_norm_w, ssd_w_out, pool_w_in, pool_mix_w, pool_scale, pool_w_out, final_norm_w):
    bsz, seq, d = x.shape
    t = bsz * seq
    depth = norm_w.shape[0]
    e = ssd_w_out.shape[1]
    n_main = 2 * e + 2 * SSD_GROUPS * SSD_STATE
    h = x.reshape(t, d)
    for i in range(depth):
        j = i // 2
        nw = norm_w[i].reshape(1, d)
        final_nw = final_norm_w.reshape(1, d) if i == depth - 1 else None
        if i % 2 == 0:
            w = ssd_w_in[j]
            gs = SSD_GROUPS * SSD_STATE
            per_group = lambda cols, width: cols.reshape(d, SSD_GROUPS, width)
            w_groups = jnp.concatenate(
                [per_group(0.5 * w[:, :e], GROUP_WIDTH), per_group(w[:, e:2 * e], GROUP_WIDTH),
                 per_group(w[:, 2 * e:2 * e + gs], SSD_STATE), per_group(w[:, 2 * e + gs:n_main], SSD_STATE)],
                axis=2).transpose(1, 0, 2).astype(bf16)
            w_dt = w[:, n_main:]
            wdt_t = (w_dt.reshape(d, 2, SSD_GROUPS, HEADS_PER_GROUP).transpose(2, 1, 3, 0)
                     .reshape(-1, d).astype(bf16))
            u, dt_t = _norm_dt(h, nw, wdt_t)
            y = _ssd_mixer(u.reshape(bsz, seq, d), w_groups, dt_t,
                           _group_major(ssd_dt_bias[j]), _group_major(ssd_a_log[j]),
                           0.5 * ssd_conv_w[j], 0.5 * ssd_conv_b[j].reshape(1, -1),
                           jnp.repeat(ssd_d[j], SSD_HEAD_DIM).reshape(1, e),
                           ssd_norm_w[j].reshape(1, e))
            h = _out_proj(y.reshape(t, e), ssd_w_out[j].astype(bf16), h, final_nw)
        else:
            gate_half = jnp.where(jnp.arange(2 * e) < e, 1.0, 0.5).astype(f32)
            proj = _in_proj(h, nw, (pool_w_in[j] * gate_half).astype(bf16))
            y = _pool_core(proj.reshape(bsz, seq, 2 * e), pool_mix_w[j].astype(bf16),
                           pool_scale[j].reshape(1, e))
            h = _out_proj(y.reshape(t, e), pool_w_out[j].astype(bf16), h, final_nw)
    return h.reshape(bsz, seq, d)
```

```python
import functools

import jax
import jax.numpy as jnp
import numpy as np
from jax import lax
from jax.experimental import pallas as pl
from jax.experimental.pallas import tpu as pltpu

f32 = jnp.float32
bf16 = jnp.bfloat16

EPS = 1e-6
LANES = 128
LOG2E = 1.4426950408889634
SSD_HEAD_DIM = 64
SSD_GROUPS = 8
SSD_STATE = 128
SSD_CONV = 5
HEADS_PER_GROUP = 4
GROUP_WIDTH = HEADS_PER_GROUP * SSD_HEAD_DIM
CHUNK = 128
HALO = 8
POOL_WINDOWS = (2, 4, 8, 16)
POOL_ROWS = 256
PROJ_ROWS = 1024

VMEM_LIMIT = 48 * 1024 * 1024
SSD_VMEM_LIMIT = 58 * 1024 * 1024


def _silu_of_twice(half):
    return half + half * jnp.tanh(half)


def _rows_ahead(win, k):
    n = win.shape[0]
    if k % 8 == 0:
        return win[k:k + n - 2 * HALO, :]
    return pltpu.roll(win, n - k, axis=0)[0:n - 2 * HALO, :]


def _normed(h_ref, nw_ref):
    xv = h_ref[...]
    ms = jnp.mean(xv * xv, axis=-1, keepdims=True)
    return (xv * lax.rsqrt(ms + EPS) * nw_ref[...]).astype(bf16)


def _in_proj_kernel(h_ref, nw_ref, w_ref, o_ref, u_scr):
    @pl.when(pl.program_id(1) == 0)
    def _():
        u_scr[...] = _normed(h_ref, nw_ref)

    o_ref[...] = jnp.dot(u_scr[...], w_ref[...], preferred_element_type=f32).astype(o_ref.dtype)


def _in_proj(h, nw, w, *, tm=1024, tn=1024):
    t, d = h.shape
    n = w.shape[1]
    return pl.pallas_call(
        _in_proj_kernel, grid=(t // tm, n // tn),
        in_specs=[pl.BlockSpec((tm, d), lambda i, j: (i, 0)),
                  pl.BlockSpec((1, d), lambda i, j: (0, 0)),
                  pl.BlockSpec((d, tn), lambda i, j: (0, j))],
        out_specs=pl.BlockSpec((tm, tn), lambda i, j: (i, j)),
        out_shape=jax.ShapeDtypeStruct((t, n), bf16),
        scratch_shapes=[pltpu.VMEM((tm, d), bf16)],
        compiler_params=pltpu.CompilerParams(dimension_semantics=("arbitrary", "arbitrary"),
                                             vmem_limit_bytes=VMEM_LIMIT),
        name="in_proj")(h, nw, w)


def _norm_dt_kernel(h_ref, nw_ref, wdt_ref, u_ref, dt_ref):
    u = _normed(h_ref, nw_ref)
    u_ref[...] = u
    dt_ref[...] = lax.dot_general(wdt_ref[...], u, (((1,), (1,)), ((), ())),
                                  preferred_element_type=f32)


def _norm_dt(h, nw, wdt_t, *, tm=1024):
    t, d = h.shape
    nd = wdt_t.shape[0]
    return pl.pallas_call(
        _norm_dt_kernel, grid=(t // tm,),
        in_specs=[pl.BlockSpec((tm, d), lambda i: (i, 0)),
                  pl.BlockSpec((1, d), lambda i: (0, 0)),
                  pl.BlockSpec((nd, d), lambda i: (0, 0))],
        out_specs=[pl.BlockSpec((tm, d), lambda i: (i, 0)), pl.BlockSpec((nd, tm), lambda i: (0, i))],
        out_shape=[jax.ShapeDtypeStruct((t, d), bf16), jax.ShapeDtypeStruct((nd, t), f32)],
        compiler_params=pltpu.CompilerParams(dimension_semantics=("arbitrary",),
                                             vmem_limit_bytes=VMEM_LIMIT),
        name="norm_dt")(h, nw, wdt_t)


def _out_proj_kernel(y_ref, w_ref, h_ref, o_ref):
    o_ref[...] = h_ref[...] + jnp.dot(y_ref[...], w_ref[...], preferred_element_type=f32)


def _out_proj_final_kernel(y_ref, w_ref, h_ref, nw_ref, o_ref):
    acc = h_ref[...] + jnp.dot(y_ref[...], w_ref[...], preferred_element_type=f32)
    ms = jnp.mean(acc * acc, axis=-1, keepdims=True)
    o_ref[...] = acc * lax.rsqrt(ms + EPS) * nw_ref[...]


def _out_proj(y, w, h, final_nw=None, *, tm=1024):
    t, e = y.shape
    d = w.shape[1]
    in_specs = [pl.BlockSpec((tm, e), lambda i: (i, 0)),
                pl.BlockSpec((e, d), lambda i: (0, 0)),
                pl.BlockSpec((tm, d), lambda i: (i, 0))]
    args = [y, w, h]
    body = _out_proj_kernel
    if final_nw is not None:
        in_specs.append(pl.BlockSpec((1, d), lambda i: (0, 0)))
        args.append(final_nw)
        body = _out_proj_final_kernel
    return pl.pallas_call(
        body, grid=(t // tm,), in_specs=in_specs,
        out_specs=pl.BlockSpec((tm, d), lambda i: (i, 0)),
        out_shape=jax.ShapeDtypeStruct((t, d), f32),
        compiler_params=pltpu.CompilerParams(dimension_semantics=("arbitrary",),
                                             vmem_limit_bytes=VMEM_LIMIT),
        name="out_proj")(*args)


def _split3(v):
    hi = v.astype(bf16).astype(f32)
    rem = v - hi
    mid = rem.astype(bf16).astype(f32)
    return hi, mid, rem - mid


_N_VALS = 24
_SPLIT_ROWS = 3 * _N_VALS
_MASKED = -1e30


_STRIP_BASE = 80
_STRIP_ROWS = 48


def _selectors():
    n_dir = 2 * HEADS_PER_GROUP
    sel_g = np.zeros((_STRIP_BASE, n_dir * CHUNK), np.float32)
    for j in range(n_dir):
        for p in range(3):
            sel_g[p * _N_VALS + j, j * CHUNK:(j + 1) * CHUNK] = 1.0 if j < HEADS_PER_GROUP else -1.0
    sels = [sel_g]
    for fwd_v, bwd_v in ((8, 12), (16, 20)):
        s = np.zeros((CHUNK, 2 * GROUP_WIDTH), np.float32)
        for a, v0 in enumerate((fwd_v, bwd_v)):
            for r in range(HEADS_PER_GROUP):
                lo = a * GROUP_WIDTH + r * SSD_HEAD_DIM
                for p in range(3):
                    s[p * _N_VALS + v0 + r, lo:lo + SSD_HEAD_DIM] = 1.0
        sels.append(s)
    return tuple(jnp.asarray(s, bf16) for s in sels)


def _ssd_kernel(u_ref, wz0_ref, wx0_ref, wb0_ref, wc0_ref, wz_ref, wx_ref, wb_ref, wc_ref,
                dt_ref, dtb_ref, alog_ref,
                cwx_ref, cwb_ref, cwc_ref, cbx_ref, cbb_ref, cbc_ref, dskip_ref, nw_ref,
                selg_ref, sels_ref, selc_ref,
                o_ref, xp_scr, xbc_scr, strip_scr, col_scr, y_scr, s_scr, edge_scr, hin_scr,
                exp_scr, cb_scr, proj_scr, z_scr):
    seq = u_ref.shape[0]
    n_chunks = seq // CHUNK
    width = GROUP_WIDTH + 2 * SSD_STATE
    step = pl.program_id(0) * pl.num_programs(1) + pl.program_id(1)
    cur = step % 2
    nxt = 1 - cur

    @pl.when(step == 0)
    def _():
        w_first = jnp.concatenate([wz0_ref[...], wx0_ref[...], wb0_ref[...], wc0_ref[...]], axis=1)
        for r0 in range(0, seq, PROJ_ROWS):
            proj_scr[0, r0:r0 + PROJ_ROWS, :] = jnp.dot(
                u_ref[r0:r0 + PROJ_ROWS, :], w_first, preferred_element_type=f32).astype(bf16)

    def project_next(q, nt):
        rows = slice(q * PROJ_ROWS, (q + 1) * PROJ_ROWS)
        cols = slice(nt * GROUP_WIDTH, (nt + 1) * GROUP_WIDTH)
        if nt == 0:
            w_tile = wz_ref[...]
        elif nt == 1:
            w_tile = wx_ref[...]
        else:
            w_tile = jnp.concatenate([wb_ref[...], wc_ref[...]], axis=1)
        proj_scr[nxt, rows, cols] = jnp.dot(u_ref[rows, :], w_tile,
                                            preferred_element_type=f32).astype(bf16)

    xp_scr[0:HALO, :] = jnp.zeros((HALO, width), f32)
    xp_scr[seq + HALO:seq + 2 * HALO, :] = jnp.zeros((HALO, width), f32)

    for c in range(n_chunks):
        src_rows = slice(c * CHUNK, (c + 1) * CHUNK)
        rows = slice(c * CHUNK + HALO, (c + 1) * CHUNK + HALO)
        xp_scr[rows, :] = proj_scr[cur, src_rows, GROUP_WIDTH:GROUP_WIDTH + width].astype(f32)
        z_scr[src_rows, :] = proj_scr[cur, src_rows, 0:GROUP_WIDTH]

    n_dir = 2 * HEADS_PER_GROUP
    stack = lambda v: jnp.concatenate([v] * n_chunks, axis=0)
    raw = (jnp.concatenate([dt_ref[:, c * CHUNK:(c + 1) * CHUNK] for c in range(n_chunks)], axis=0)
           + stack(dtb_ref[...]))
    is_fwd = (lax.broadcasted_iota(jnp.int32, raw.shape, 0) & (n_dir - 1)) < HEADS_PER_GROUP
    dt = jnp.maximum(raw, 0.0) + jnp.log1p(jnp.exp(-jnp.abs(raw)))
    a2 = -(dt * stack(jnp.exp(alog_ref[...]) * LOG2E))
    tri = (lax.broadcasted_iota(jnp.int32, (CHUNK, CHUNK), 0)
           <= lax.broadcasted_iota(jnp.int32, (CHUNK, CHUNK), 1)).astype(bf16)
    sums = jnp.dot(jnp.concatenate(_split3(a2), axis=0).astype(bf16), tri, preferred_element_type=f32)
    n_rows = raw.shape[0]
    incl = sums[0:n_rows] + sums[n_rows:2 * n_rows] + sums[2 * n_rows:3 * n_rows]
    tot = incl[:, CHUNK - 1:CHUNK]
    cs = jnp.where(is_fwd, incl, incl - a2)
    log2_dt = jnp.maximum(jnp.log2(dt), -1e30)
    src_parts = _split3(jnp.where(is_fwd, log2_dt - cs, cs + log2_dt))
    to_end = jnp.exp2(tot - cs)
    from_start = jnp.exp2(cs)
    state_w = jnp.where(is_fwd, to_end, from_start) * dt
    carry_w = jnp.where(is_fwd, from_start, to_end)
    val_parts = list(zip(_split3(cs), _split3(state_w), _split3(carry_w)))

    own_block = (lax.broadcasted_iota(jnp.int32, (8, n_dir * CHUNK), 1) // CHUNK
                 == lax.broadcasted_iota(jnp.int32, (8, n_dir * CHUNK), 0))
    eye = (lax.broadcasted_iota(jnp.int32, (CHUNK, CHUNK), 0)
           == lax.broadcasted_iota(jnp.int32, (CHUNK, CHUNK), 1)).astype(bf16)
    ones_rows = jnp.ones((CHUNK - _SPLIT_ROWS, CHUNK), f32)
    zero_rows = jnp.zeros((8, n_dir * CHUNK), f32)
    contract1 = (((1,), (1,)), ((), ()))
    for c in range(n_chunks):
        own = slice(c * n_dir, (c + 1) * n_dir)
        rows = jnp.concatenate([v[own] for part in val_parts for v in part] + [ones_rows],
                               axis=0).astype(bf16)
        col_scr[c * CHUNK:(c + 1) * CHUNK, :] = lax.dot_general(
            eye, rows, contract1, preferred_element_type=f32).astype(bf16)
        strip = []
        for part in src_parts:
            tiled = jnp.concatenate([part[own]] * n_dir, axis=1)
            strip += [jnp.where(own_block, tiled, 0.0), zero_rows]
        strip_scr[c] = jnp.concatenate(strip, axis=0).astype(bf16)
    edges = jnp.concatenate(
        [col_scr[r0:r0 + 16, :] for c in range(n_chunks) for r0 in ((c + 1) * CHUNK - 16, c * CHUNK)],
        axis=0)
    edge_all = jnp.dot(edges, selc_ref[...], preferred_element_type=f32)
    for c in range(n_chunks):
        edge_scr[c] = edge_all[c * 32:(c + 1) * 32]

    conv_w = jnp.concatenate([cwx_ref[...], cwb_ref[...], cwc_ref[...]], axis=1)
    conv_b = jnp.concatenate([cbx_ref[...], cbb_ref[...], cbc_ref[...]], axis=1)
    dskip = dskip_ref[...]
    nw = nw_ref[...]
    lane_head = lax.broadcasted_iota(jnp.int32, (CHUNK, GROUP_WIDTH), 1) // SSD_HEAD_DIM
    contract0 = (((0,), (0,)), ((), ()))

    chunks_per_tile = PROJ_ROWS // CHUNK
    n_col_tiles = (GROUP_WIDTH + width) // GROUP_WIDTH
    tile_at = {}
    for q in range(seq // PROJ_ROWS):
        slots = [(q * chunks_per_tile + i, k) for i in range(chunks_per_tile) for k in (1, 3)]
        for nt in range(n_col_tiles):
            tile_at[slots[nt * len(slots) // n_col_tiles]] = (q, nt)

    def local_pass(c):
        base = c * CHUNK
        win = xp_scr[pl.ds(base, CHUNK + 2 * HALO), :]
        acc = jnp.broadcast_to(conv_b, (CHUNK, width))
        for k in range(SSD_CONV):
            acc = acc + _rows_ahead(win, HALO - SSD_CONV // 2 + k) * conv_w[k:k + 1, :]
            if (c, k) in tile_at:
                project_next(*tile_at[(c, k)])
        xbc = _silu_of_twice(acc)
        xbc_scr[pl.ds(base, CHUNK), :] = xbc.astype(bf16)
        xs = xbc[:, 0:GROUP_WIDTH]
        bm = xbc[:, GROUP_WIDTH:GROUP_WIDTH + SSD_STATE].astype(bf16)
        cm = xbc[:, GROUP_WIDTH + SSD_STATE:width].astype(bf16)
        y_scr[pl.ds(base, CHUNK), :] = xs * dskip
        col = col_scr[pl.ds(base, CHUNK), :]
        exp_scr[c] = jnp.dot(col, jnp.concatenate([selg_ref[...], strip_scr[c]], axis=0),
                             preferred_element_type=f32)
        cb_scr[c] = lax.dot_general(cm, bm, contract1, preferred_element_type=f32)
        state_w = jnp.dot(col, sels_ref[...], preferred_element_type=f32)
        weighted = jnp.concatenate([(xs * state_w[:, 0:GROUP_WIDTH]).astype(bf16),
                                    (xs * state_w[:, GROUP_WIDTH:]).astype(bf16)], axis=1)
        s_scr[c] = lax.dot_general(bm, weighted, contract0, preferred_element_type=f32)

    for c in range(n_chunks):
        local_pass(c)

    def recur(i, carry):
        hf, hb = carry
        cf = i
        cr = n_chunks - 1 - i
        hin_scr[cf, :, 0:GROUP_WIDTH] = hf.astype(bf16)
        hin_scr[cr, :, GROUP_WIDTH:2 * GROUP_WIDTH] = hb.astype(bf16)
        hf = edge_scr[cf][15:16, 0:GROUP_WIDTH] * hf + s_scr[cf][:, 0:GROUP_WIDTH]
        hb = edge_scr[cr][16:17, GROUP_WIDTH:2 * GROUP_WIDTH] * hb + s_scr[cr][:, GROUP_WIDTH:2 * GROUP_WIDTH]
        return hf, hb

    h0 = jnp.zeros((SSD_STATE, GROUP_WIDTH), f32)
    lax.fori_loop(0, n_chunks, recur, (h0, h0))


    li = lax.broadcasted_iota(jnp.int32, (CHUNK, CHUNK), 0)
    si = lax.broadcasted_iota(jnp.int32, (CHUNK, CHUNK), 1)
    lower = si <= li
    upper = si >= li

    def output_pass(c):
        base = c * CHUNK
        xbc = xbc_scr[pl.ds(base, CHUNK), :]
        xs = xbc[:, 0:GROUP_WIDTH]
        cm = xbc[:, GROUP_WIDTH + SSD_STATE:width]
        col = col_scr[pl.ds(base, CHUNK), :]
        carried = (jnp.dot(col, selc_ref[...], preferred_element_type=f32)
                   * jnp.dot(cm, hin_scr[c], preferred_element_type=f32))
        cb = cb_scr[c]
        g_heads, x_heads = [], []
        for r in range(HEADS_PER_GROUP):
            arg_f = exp_scr[c, :, r * CHUNK:(r + 1) * CHUNK]
            arg_b = exp_scr[c, :, (HEADS_PER_GROUP + r) * CHUNK:(HEADS_PER_GROUP + r + 1) * CHUNK]
            decay = (jnp.exp2(jnp.where(lower, arg_f, _MASKED))
                     + jnp.exp2(jnp.where(upper, arg_b, _MASKED)))
            g_heads.append((cb * decay).astype(bf16))
            x_heads.append(jnp.where(lane_head == r, xs, jnp.zeros_like(xs)))
        y = y_scr[pl.ds(base, CHUNK), :] + jnp.dot(
            jnp.concatenate(g_heads, axis=1), jnp.concatenate(x_heads, axis=0),
            preferred_element_type=f32)
        y = y + carried[:, 0:GROUP_WIDTH] + carried[:, GROUP_WIDTH:]
        y = y * _silu_of_twice(z_scr[pl.ds(base, CHUNK), :].astype(f32))
        ms = jnp.mean(y * y, axis=-1, keepdims=True)
        o_ref[pl.ds(base, CHUNK), :] = (y * lax.rsqrt(ms + EPS) * nw).astype(o_ref.dtype)

    for c in range(n_chunks):
        output_pass(c)


def _ssd_mixer(u, w_main, dt_t, dt_bias, a_log, conv_w, conv_b, dskip, norm_w):
    bsz, seq, d = u.shape
    e = SSD_GROUPS * GROUP_WIDTH
    width = GROUP_WIDTH + 2 * SSD_STATE
    n_proj = GROUP_WIDTH + width
    next_b = lambda b, g: jnp.minimum(b + (g + 1) // SSD_GROUPS, bsz - 1)
    next_g = lambda g: (g + 1) % SSD_GROUPS
    w_specs = lambda grp: [
        pl.BlockSpec((d, GROUP_WIDTH), lambda b, g: (0, grp(g))),
        pl.BlockSpec((d, GROUP_WIDTH), lambda b, g: (0, e // GROUP_WIDTH + grp(g))),
        pl.BlockSpec((d, SSD_STATE), lambda b, g: (0, 2 * e // SSD_STATE + grp(g))),
        pl.BlockSpec((d, SSD_STATE), lambda b, g: (0, 2 * e // SSD_STATE + SSD_GROUPS + grp(g))),
    ]
    in_specs = [pl.BlockSpec((None, seq, d), lambda b, g: (next_b(b, g), 0, 0))]
    in_specs += w_specs(lambda g: 0)
    in_specs += w_specs(next_g)
    in_specs += [
        pl.BlockSpec((8, seq), lambda b, g: (g, b)),
        pl.BlockSpec((8, 1), lambda b, g: (g, 0)),
        pl.BlockSpec((8, 1), lambda b, g: (g, 0)),
        pl.BlockSpec((SSD_CONV, GROUP_WIDTH), lambda b, g: (0, g)),
        pl.BlockSpec((SSD_CONV, SSD_STATE), lambda b, g: (0, e // SSD_STATE + g)),
        pl.BlockSpec((SSD_CONV, SSD_STATE), lambda b, g: (0, e // SSD_STATE + SSD_GROUPS + g)),
        pl.BlockSpec((1, GROUP_WIDTH), lambda b, g: (0, g)),
        pl.BlockSpec((1, SSD_STATE), lambda b, g: (0, e // SSD_STATE + g)),
        pl.BlockSpec((1, SSD_STATE), lambda b, g: (0, e // SSD_STATE + SSD_GROUPS + g)),
        pl.BlockSpec((1, GROUP_WIDTH), lambda b, g: (0, g)),
        pl.BlockSpec((1, GROUP_WIDTH), lambda b, g: (0, g)),
    ]
    selectors = _selectors()
    in_specs += [pl.BlockSpec(s.shape, lambda b, g: (0, 0)) for s in selectors]
    return pl.pallas_call(
        _ssd_kernel, grid=(bsz, SSD_GROUPS), in_specs=in_specs,
        out_specs=pl.BlockSpec((None, seq, GROUP_WIDTH), lambda b, g: (b, 0, g)),
        out_shape=jax.ShapeDtypeStruct((bsz, seq, e), bf16),
        scratch_shapes=[
            pltpu.VMEM((seq + 2 * HALO, width), f32),
            pltpu.VMEM((seq, width), bf16),
            pltpu.VMEM((seq // CHUNK, _STRIP_ROWS, 2 * HEADS_PER_GROUP * CHUNK), bf16),
            pltpu.VMEM((seq, CHUNK), bf16),
            pltpu.VMEM((seq, GROUP_WIDTH), f32),
            pltpu.VMEM((seq // CHUNK, SSD_STATE, 2 * GROUP_WIDTH), f32),
            pltpu.VMEM((seq // CHUNK, 32, 2 * GROUP_WIDTH), f32),
            pltpu.VMEM((seq // CHUNK, SSD_STATE, 2 * GROUP_WIDTH), bf16),
            pltpu.VMEM((seq // CHUNK, CHUNK, 2 * HEADS_PER_GROUP * CHUNK), f32),
            pltpu.VMEM((seq // CHUNK, CHUNK, CHUNK), f32),
            pltpu.VMEM((2, seq, n_proj), bf16),
            pltpu.VMEM((seq, GROUP_WIDTH), bf16),
        ],
        compiler_params=pltpu.CompilerParams(dimension_semantics=("arbitrary", "arbitrary"),
                                             vmem_limit_bytes=SSD_VMEM_LIMIT),
        name="ssd_mixer")(u, *([w_main] * 8), dt_t, dt_bias, a_log,
                          conv_w, conv_w, conv_w, conv_b, conv_b, conv_b, dskip, norm_w,
                          *selectors)


def _window_sum(win, w):
    half = w // 2
    if w == 2:
        return _rows_ahead(win, HALO - 1) + _rows_ahead(win, HALO)
    t = win + pltpu.roll(win, win.shape[0] - 1, axis=0)
    span = 2
    while span * 2 < w:
        t = t + pltpu.roll(t, win.shape[0] - span, axis=0)
        span *= 2
    return _rows_ahead(t, HALO - half) + _rows_ahead(t, HALO)


def _pool_kernel(v_ref, gate_ref, mix_ref, scale_ref, o_ref, vp_scr):
    seq, gd = v_ref.shape
    n_steps = seq // POOL_ROWS
    gi = pl.program_id(1)

    vp_scr[0:HALO, :] = jnp.zeros((HALO, gd), f32)
    vp_scr[seq + HALO:seq + 2 * HALO, :] = jnp.zeros((HALO, gd), f32)

    def fill(c, carry):
        base = pl.multiple_of(c * POOL_ROWS, POOL_ROWS)
        vp_scr[pl.ds(base + HALO, POOL_ROWS), :] = v_ref[pl.ds(base, POOL_ROWS), :].astype(f32)
        return carry

    lax.fori_loop(0, n_steps, fill, 0)
    scale = scale_ref[...]

    for k, w in enumerate(POOL_WINDOWS):
        @pl.when(gi == k)
        def _(w=w):
            half = w // 2

            def step(c, carry):
                base = pl.multiple_of(c * POOL_ROWS, POOL_ROWS)
                win = vp_scr[pl.ds(base, POOL_ROWS + 2 * HALO), :]
                pos = base + lax.broadcasted_iota(jnp.int32, (POOL_ROWS, 1), 0)
                cnt = (jnp.minimum(pos + half, seq) - jnp.maximum(pos - half, 0)).astype(f32)
                pooled = _window_sum(win, w) / cnt - win[HALO:HALO + POOL_ROWS, :]
                mixed = jnp.dot(pooled.astype(bf16), mix_ref[...], preferred_element_type=f32)
                gate = gate_ref[pl.ds(base, POOL_ROWS), :].astype(f32)
                o_ref[pl.ds(base, POOL_ROWS), :] = (mixed * scale * _silu_of_twice(gate)).astype(o_ref.dtype)
                return carry

            lax.fori_loop(0, n_steps, step, 0)


def _pool_core(proj, mix_w, scale):
    bsz, seq, e2 = proj.shape
    e = e2 // 2
    ng = len(POOL_WINDOWS)
    gd = e // ng
    return pl.pallas_call(
        _pool_kernel, grid=(bsz, ng),
        in_specs=[pl.BlockSpec((None, seq, gd), lambda b, g: (b, 0, g)),
                  pl.BlockSpec((None, seq, gd), lambda b, g: (b, 0, ng + g)),
                  pl.BlockSpec((None, gd, gd), lambda b, g: (g, 0, 0)),
                  pl.BlockSpec((1, gd), lambda b, g: (0, g))],
        out_specs=pl.BlockSpec((None, seq, gd), lambda b, g: (b, 0, g)),
        out_shape=jax.ShapeDtypeStruct((bsz, seq, e), bf16),
        scratch_shapes=[pltpu.VMEM((seq + 2 * HALO, gd), f32)],
        compiler_params=pltpu.CompilerParams(dimension_semantics=("arbitrary", "arbitrary"),
                                             vmem_limit_bytes=VMEM_LIMIT),
        name="pool_core")(proj, proj, mix_w, scale)


def _group_major(p):
    return p.reshape(2, SSD_GROUPS, HEADS_PER_GROUP).transpose(1, 0, 2).reshape(-1, 1)


def kernel(x, norm_w, ssd_w_in, ssd_conv_w, ssd_conv_b, ssd_dt_bias, ssd_a_log, ssd_d, ssd_norm_w, ssd_w_out, pool_w_in, pool_mix_w, pool_scale, pool_w_out, final_norm_w):
    bsz, seq, d = x.shape
    t = bsz * seq
    depth = norm_w.shape[0]
    e = ssd_w_out.shape[1]
    n_main = 2 * e + 2 * SSD_GROUPS * SSD_STATE
    h = x.reshape(t, d)
    for i in range(depth):
        j = i // 2
        nw = norm_w[i].reshape(1, d)
        final_nw = final_norm_w.reshape(1, d) if i == depth - 1 else None
        if i % 2 == 0:
            w = ssd_w_in[j]
            z_half = jnp.where(jnp.arange(n_main) < e, 0.5, 1.0).astype(f32)
            w_main = (w[:, :n_main] * z_half).astype(bf16)
            w_dt = w[:, n_main:]
            wdt_t = (w_dt.reshape(d, 2, SSD_GROUPS, HEADS_PER_GROUP).transpose(2, 1, 3, 0)
                     .reshape(-1, d).astype(bf16))
            u, dt_t = _norm_dt(h, nw, wdt_t)
            y = _ssd_mixer(u.reshape(bsz, seq, d), w_main, dt_t,
                           _group_major(ssd_dt_bias[j]), _group_major(ssd_a_log[j]),
                           0.5 * ssd_conv_w[j], 0.5 * ssd_conv_b[j].reshape(1, -1),
                           jnp.repeat(ssd_d[j], SSD_HEAD_DIM).reshape(1, e),
                           ssd_norm_w[j].reshape(1, e))
            h = _out_proj(y.reshape(t, e), ssd_w_out[j].astype(bf16), h, final_nw)
        else:
            gate_half = jnp.where(jnp.arange(2 * e) < e, 1.0, 0.5).astype(f32)
            proj = _in_proj(h, nw, (pool_w_in[j] * gate_half).astype(bf16))
            y = _pool_core(proj.reshape(bsz, seq, 2 * e), pool_mix_w[j].astype(bf16),
                           pool_scale[j].reshape(1, e))
            h = _out_proj(y.reshape(t, e), pool_w_out[j].astype(bf16), h, final_nw)
    return h.reshape(bsz, seq, d)
```

```python
import functools

import jax
import jax.numpy as jnp
import numpy as np
from jax import lax
from jax.experimental import pallas as pl
from jax.experimental.pallas import tpu as pltpu

f32 = jnp.float32
bf16 = jnp.bfloat16

EPS = 1e-6
LANES = 128
LOG2E = 1.4426950408889634
SSD_HEAD_DIM = 64
SSD_GROUPS = 8
SSD_STATE = 128
SSD_CONV = 5
HEADS_PER_GROUP = 4
GROUP_WIDTH = HEADS_PER_GROUP * SSD_HEAD_DIM
CHUNK = 128
HALO = 8
POOL_WINDOWS = (2, 4, 8, 16)
POOL_ROWS = 256
PROJ_ROWS = 1024

VMEM_LIMIT = 48 * 1024 * 1024
SSD_VMEM_LIMIT = 58 * 1024 * 1024


def _silu_of_twice(half):
    return half + half * jnp.tanh(half)


def _rows_ahead(win, k):
    n = win.shape[0]
    if k % 8 == 0:
        return win[k:k + n - 2 * HALO, :]
    return pltpu.roll(win, n - k, axis=0)[0:n - 2 * HALO, :]


def _normed(h_ref, nw_ref):
    xv = h_ref[...]
    ms = jnp.mean(xv * xv, axis=-1, keepdims=True)
    return (xv * lax.rsqrt(ms + EPS) * nw_ref[...]).astype(bf16)


def _in_proj_kernel(h_ref, nw_ref, w_ref, o_ref, u_scr, *, first_halved_tile):
    @pl.when(pl.program_id(1) == 0)
    def _():
        u_scr[...] = _normed(h_ref, nw_ref)

    acc = jnp.dot(u_scr[...], w_ref[...].astype(bf16), preferred_element_type=f32)
    acc = acc * jnp.where(pl.program_id(1) >= first_halved_tile, 0.5, 1.0)
    o_ref[...] = acc.astype(o_ref.dtype)


def _in_proj(h, nw, w_layers, layer, n_plain, *, tm=1024, tn=1024):
    t, d = h.shape
    n = w_layers.shape[2]
    return pl.pallas_call(
        functools.partial(_in_proj_kernel, first_halved_tile=n_plain // tn), grid=(t // tm, n // tn),
        in_specs=[pl.BlockSpec((tm, d), lambda i, j: (i, 0)),
                  pl.BlockSpec((1, d), lambda i, j: (0, 0)),
                  pl.BlockSpec((None, d, tn), lambda i, j: (layer, 0, j))],
        out_specs=pl.BlockSpec((tm, tn), lambda i, j: (i, j)),
        out_shape=jax.ShapeDtypeStruct((t, n), bf16),
        scratch_shapes=[pltpu.VMEM((tm, d), bf16)],
        compiler_params=pltpu.CompilerParams(dimension_semantics=("arbitrary", "arbitrary"),
                                             vmem_limit_bytes=VMEM_LIMIT),
        name="in_proj")(h, nw, w_layers)


def _norm_dt_kernel(h_ref, nw_ref, wdt_ref, u_ref, dt_ref):
    u = _normed(h_ref, nw_ref)
    u_ref[...] = u
    dt_ref[...] = lax.dot_general(wdt_ref[...], u, (((1,), (1,)), ((), ())),
                                  preferred_element_type=f32)


def _norm_dt(h, nw, wdt_t, *, tm=1024):
    t, d = h.shape
    nd = wdt_t.shape[0]
    return pl.pallas_call(
        _norm_dt_kernel, grid=(t // tm,),
        in_specs=[pl.BlockSpec((tm, d), lambda i: (i, 0)),
                  pl.BlockSpec((1, d), lambda i: (0, 0)),
                  pl.BlockSpec((nd, d), lambda i: (0, 0))],
        out_specs=[pl.BlockSpec((tm, d), lambda i: (i, 0)), pl.BlockSpec((nd, tm), lambda i: (0, i))],
        out_shape=[jax.ShapeDtypeStruct((t, d), bf16), jax.ShapeDtypeStruct((nd, t), f32)],
        compiler_params=pltpu.CompilerParams(dimension_semantics=("arbitrary",),
                                             vmem_limit_bytes=VMEM_LIMIT),
        name="norm_dt")(h, nw, wdt_t)


def _cast_weight_once(w_ref, w_scr):
    @pl.when(pl.program_id(0) == 0)
    def _():
        w_scr[...] = w_ref[...].astype(bf16)


def _out_proj_kernel(y_ref, w_ref, h_ref, o_ref, w_scr):
    _cast_weight_once(w_ref, w_scr)
    o_ref[...] = h_ref[...] + jnp.dot(y_ref[...], w_scr[...], preferred_element_type=f32)


def _out_proj_final_kernel(y_ref, w_ref, h_ref, nw_ref, o_ref, w_scr):
    _cast_weight_once(w_ref, w_scr)
    acc = h_ref[...] + jnp.dot(y_ref[...], w_scr[...], preferred_element_type=f32)
    ms = jnp.mean(acc * acc, axis=-1, keepdims=True)
    o_ref[...] = acc * lax.rsqrt(ms + EPS) * nw_ref[...]


def _out_proj(y, w_layers, layer, h, final_nw=None, *, tm=1024):
    t, e = y.shape
    d = w_layers.shape[2]
    in_specs = [pl.BlockSpec((tm, e), lambda i: (i, 0)),
                pl.BlockSpec((None, e, d), lambda i: (layer, 0, 0)),
                pl.BlockSpec((tm, d), lambda i: (i, 0))]
    args = [y, w_layers, h]
    body = _out_proj_kernel
    if final_nw is not None:
        in_specs.append(pl.BlockSpec((1, d), lambda i: (0, 0)))
        args.append(final_nw)
        body = _out_proj_final_kernel
    return pl.pallas_call(
        body, grid=(t // tm,), in_specs=in_specs,
        out_specs=pl.BlockSpec((tm, d), lambda i: (i, 0)),
        out_shape=jax.ShapeDtypeStruct((t, d), f32),
        scratch_shapes=[pltpu.VMEM((e, d), bf16)],
        compiler_params=pltpu.CompilerParams(dimension_semantics=("arbitrary",),
                                             vmem_limit_bytes=VMEM_LIMIT),
        name="out_proj")(*args)


def _split3(v):
    hi = v.astype(bf16).astype(f32)
    rem = v - hi
    mid = rem.astype(bf16).astype(f32)
    return hi, mid, rem - mid


_N_VALS = 24
_SPLIT_ROWS = 3 * _N_VALS
_MASKED = -1e30


_STRIP_BASE = 80
_STRIP_ROWS = 48


def _selectors():
    n_dir = 2 * HEADS_PER_GROUP
    sel_g = np.zeros((_STRIP_BASE, n_dir * CHUNK), np.float32)
    for j in range(n_dir):
        for p in range(3):
            sel_g[p * _N_VALS + j, j * CHUNK:(j + 1) * CHUNK] = 1.0 if j < HEADS_PER_GROUP else -1.0
    sels = [sel_g]
    for fwd_v, bwd_v in ((8, 12), (16, 20)):
        s = np.zeros((CHUNK, 2 * GROUP_WIDTH), np.float32)
        for a, v0 in enumerate((fwd_v, bwd_v)):
            for r in range(HEADS_PER_GROUP):
                lo = a * GROUP_WIDTH + r * SSD_HEAD_DIM
                for p in range(3):
                    s[p * _N_VALS + v0 + r, lo:lo + SSD_HEAD_DIM] = 1.0
        sels.append(s)
    return tuple(jnp.asarray(s, bf16) for s in sels)


def _ssd_kernel(u_ref, wz0_ref, wx0_ref, wb0_ref, wc0_ref, wz_ref, wx_ref, wb_ref, wc_ref,
                dt_ref, dtb_ref, alog_ref,
                cwx_ref, cwb_ref, cwc_ref, cbx_ref, cbb_ref, cbc_ref, dskip_ref, nw_ref,
                selg_ref, sels_ref, selc_ref,
                o_ref, xp_scr, xbc_scr, strip_scr, col_scr, y_scr, s_scr, edge_scr, hin_scr,
                exp_scr, cb_scr, proj_scr, z_scr):
    seq = u_ref.shape[0]
    n_chunks = seq // CHUNK
    width = GROUP_WIDTH + 2 * SSD_STATE
    step = pl.program_id(0) * pl.num_programs(1) + pl.program_id(1)
    cur = step % 2
    nxt = 1 - cur

    def project(slot, rows, w_refs, nt):
        wz, wx, wb, wc = w_refs
        if nt == 0:
            w_tile = wz[...].astype(bf16)
        elif nt == 1:
            w_tile = wx[...].astype(bf16)
        else:
            w_tile = jnp.concatenate([wb[...], wc[...]], axis=1).astype(bf16)
        acc = jnp.dot(u_ref[rows, :], w_tile, preferred_element_type=f32)
        if nt == 0:
            acc = 0.5 * acc
        proj_scr[slot, rows, nt * GROUP_WIDTH:(nt + 1) * GROUP_WIDTH] = acc.astype(bf16)

    def project_next(q, nt):
        project(nxt, slice(q * PROJ_ROWS, (q + 1) * PROJ_ROWS), (wz_ref, wx_ref, wb_ref, wc_ref), nt)

    @pl.when(step == 0)
    def _():
        for r0 in range(0, seq, PROJ_ROWS):
            for nt in range((GROUP_WIDTH + width) // GROUP_WIDTH):
                project(0, slice(r0, r0 + PROJ_ROWS), (wz0_ref, wx0_ref, wb0_ref, wc0_ref), nt)

    xp_scr[0:HALO, :] = jnp.zeros((HALO, width), f32)
    xp_scr[seq + HALO:seq + 2 * HALO, :] = jnp.zeros((HALO, width), f32)

    for c in range(n_chunks):
        src_rows = slice(c * CHUNK, (c + 1) * CHUNK)
        rows = slice(c * CHUNK + HALO, (c + 1) * CHUNK + HALO)
        xp_scr[rows, :] = proj_scr[cur, src_rows, GROUP_WIDTH:GROUP_WIDTH + width].astype(f32)
        z_scr[src_rows, :] = proj_scr[cur, src_rows, 0:GROUP_WIDTH]

    n_dir = 2 * HEADS_PER_GROUP
    stack = lambda v: jnp.concatenate([v] * n_chunks, axis=0)
    raw = (jnp.concatenate([dt_ref[:, c * CHUNK:(c + 1) * CHUNK] for c in range(n_chunks)], axis=0)
           + stack(dtb_ref[...]))
    is_fwd = (lax.broadcasted_iota(jnp.int32, raw.shape, 0) & (n_dir - 1)) < HEADS_PER_GROUP
    dt = jnp.maximum(raw, 0.0) + jnp.log1p(jnp.exp(-jnp.abs(raw)))
    a2 = -(dt * stack(jnp.exp(alog_ref[...]) * LOG2E))
    tri = (lax.broadcasted_iota(jnp.int32, (CHUNK, CHUNK), 0)
           <= lax.broadcasted_iota(jnp.int32, (CHUNK, CHUNK), 1)).astype(bf16)
    sums = jnp.dot(jnp.concatenate(_split3(a2), axis=0).astype(bf16), tri, preferred_element_type=f32)
    n_rows = raw.shape[0]
    incl = sums[0:n_rows] + sums[n_rows:2 * n_rows] + sums[2 * n_rows:3 * n_rows]
    tot = incl[:, CHUNK - 1:CHUNK]
    cs = jnp.where(is_fwd, incl, incl - a2)
    log2_dt = jnp.maximum(jnp.log2(dt), -1e30)
    src_parts = _split3(jnp.where(is_fwd, log2_dt - cs, cs + log2_dt))
    to_end = jnp.exp2(tot - cs)
    from_start = jnp.exp2(cs)
    state_w = jnp.where(is_fwd, to_end, from_start) * dt
    carry_w = jnp.where(is_fwd, from_start, to_end)
    val_parts = list(zip(_split3(cs), _split3(state_w), _split3(carry_w)))

    own_block = (lax.broadcasted_iota(jnp.int32, (8, n_dir * CHUNK), 1) // CHUNK
                 == lax.broadcasted_iota(jnp.int32, (8, n_dir * CHUNK), 0))
    eye = (lax.broadcasted_iota(jnp.int32, (CHUNK, CHUNK), 0)
           == lax.broadcasted_iota(jnp.int32, (CHUNK, CHUNK), 1)).astype(bf16)
    ones_rows = jnp.ones((CHUNK - _SPLIT_ROWS, CHUNK), f32)
    zero_rows = jnp.zeros((8, n_dir * CHUNK), f32)
    contract1 = (((1,), (1,)), ((), ()))
    for c in range(n_chunks):
        own = slice(c * n_dir, (c + 1) * n_dir)
        rows = jnp.concatenate([v[own] for part in val_parts for v in part] + [ones_rows],
                               axis=0).astype(bf16)
        col_scr[c * CHUNK:(c + 1) * CHUNK, :] = lax.dot_general(
            eye, rows, contract1, preferred_element_type=f32).astype(bf16)
        strip = []
        for part in src_parts:
            tiled = jnp.concatenate([part[own]] * n_dir, axis=1)
            strip += [jnp.where(own_block, tiled, 0.0), zero_rows]
        strip_scr[c] = jnp.concatenate(strip, axis=0).astype(bf16)
    edges = jnp.concatenate(
        [col_scr[r0:r0 + 16, :] for c in range(n_chunks) for r0 in ((c + 1) * CHUNK - 16, c * CHUNK)],
        axis=0)
    edge_all = jnp.dot(edges, selc_ref[...], preferred_element_type=f32)
    for c in range(n_chunks):
        edge_scr[c] = edge_all[c * 32:(c + 1) * 32]

    conv_w = jnp.concatenate([cwx_ref[...], cwb_ref[...], cwc_ref[...]], axis=1)
    conv_b = jnp.concatenate([cbx_ref[...], cbb_ref[...], cbc_ref[...]], axis=1)
    dskip = dskip_ref[...]
    nw = nw_ref[...]
    lane_head = lax.broadcasted_iota(jnp.int32, (CHUNK, GROUP_WIDTH), 1) // SSD_HEAD_DIM
    contract0 = (((0,), (0,)), ((), ()))

    chunks_per_tile = PROJ_ROWS // CHUNK
    n_col_tiles = (GROUP_WIDTH + width) // GROUP_WIDTH
    tile_at = {}
    for q in range(seq // PROJ_ROWS):
        slots = [(q * chunks_per_tile + i, k) for i in range(chunks_per_tile) for k in (1, 3)]
        for nt in range(n_col_tiles):
            tile_at[slots[nt * len(slots) // n_col_tiles]] = (q, nt)

    def local_pass(c):
        base = c * CHUNK
        win = xp_scr[pl.ds(base, CHUNK + 2 * HALO), :]
        acc = jnp.broadcast_to(conv_b, (CHUNK, width))
        for k in range(SSD_CONV):
            acc = acc + _rows_ahead(win, HALO - SSD_CONV // 2 + k) * conv_w[k:k + 1, :]
            if (c, k) in tile_at:
                project_next(*tile_at[(c, k)])
        xbc = _silu_of_twice(acc)
        xbc_scr[pl.ds(base, CHUNK), :] = xbc.astype(bf16)
        xs = xbc[:, 0:GROUP_WIDTH]
        bm = xbc[:, GROUP_WIDTH:GROUP_WIDTH + SSD_STATE].astype(bf16)
        cm = xbc[:, GROUP_WIDTH + SSD_STATE:width].astype(bf16)
        y_scr[pl.ds(base, CHUNK), :] = xs * dskip
        col = col_scr[pl.ds(base, CHUNK), :]
        exp_scr[c] = jnp.dot(col, jnp.concatenate([selg_ref[...], strip_scr[c]], axis=0),
                             preferred_element_type=f32)
        cb_scr[c] = lax.dot_general(cm, bm, contract1, preferred_element_type=f32)
        state_w = jnp.dot(col, sels_ref[...], preferred_element_type=f32)
        weighted = jnp.concatenate([(xs * state_w[:, 0:GROUP_WIDTH]).astype(bf16),
                                    (xs * state_w[:, GROUP_WIDTH:]).astype(bf16)], axis=1)
        s_scr[c] = lax.dot_general(bm, weighted, contract0, preferred_element_type=f32)

    for c in range(n_chunks):
        local_pass(c)

    def recur(i, carry):
        hf, hb = carry
        cf = i
        cr = n_chunks - 1 - i
        hin_scr[cf, :, 0:GROUP_WIDTH] = hf.astype(bf16)
        hin_scr[cr, :, GROUP_WIDTH:2 * GROUP_WIDTH] = hb.astype(bf16)
        hf = edge_scr[cf][15:16, 0:GROUP_WIDTH] * hf + s_scr[cf][:, 0:GROUP_WIDTH]
        hb = edge_scr[cr][16:17, GROUP_WIDTH:2 * GROUP_WIDTH] * hb + s_scr[cr][:, GROUP_WIDTH:2 * GROUP_WIDTH]
        return hf, hb

    h0 = jnp.zeros((SSD_STATE, GROUP_WIDTH), f32)
    lax.fori_loop(0, n_chunks, recur, (h0, h0))


    li = lax.broadcasted_iota(jnp.int32, (CHUNK, CHUNK), 0)
    si = lax.broadcasted_iota(jnp.int32, (CHUNK, CHUNK), 1)
    lower = si <= li
    upper = si >= li

    def output_pass(c):
        base = c * CHUNK
        xbc = xbc_scr[pl.ds(base, CHUNK), :]
        xs = xbc[:, 0:GROUP_WIDTH]
        cm = xbc[:, GROUP_WIDTH + SSD_STATE:width]
        col = col_scr[pl.ds(base, CHUNK), :]
        carried = (jnp.dot(col, selc_ref[...], preferred_element_type=f32)
                   * jnp.dot(cm, hin_scr[c], preferred_element_type=f32))
        cb = cb_scr[c]
        g_heads, x_heads = [], []
        for r in range(HEADS_PER_GROUP):
            arg_f = exp_scr[c, :, r * CHUNK:(r + 1) * CHUNK]
            arg_b = exp_scr[c, :, (HEADS_PER_GROUP + r) * CHUNK:(HEADS_PER_GROUP + r + 1) * CHUNK]
            decay = (jnp.exp2(jnp.where(lower, arg_f, _MASKED))
                     + jnp.exp2(jnp.where(upper, arg_b, _MASKED)))
            g_heads.append((cb * decay).astype(bf16))
            x_heads.append(jnp.where(lane_head == r, xs, jnp.zeros_like(xs)))
        y = y_scr[pl.ds(base, CHUNK), :] + jnp.dot(
            jnp.concatenate(g_heads, axis=1), jnp.concatenate(x_heads, axis=0),
            preferred_element_type=f32)
        y = y + carried[:, 0:GROUP_WIDTH] + carried[:, GROUP_WIDTH:]
        y = y * _silu_of_twice(z_scr[pl.ds(base, CHUNK), :].astype(f32))
        ms = jnp.mean(y * y, axis=-1, keepdims=True)
        o_ref[pl.ds(base, CHUNK), :] = (y * lax.rsqrt(ms + EPS) * nw).astype(o_ref.dtype)

    for c in range(n_chunks):
        output_pass(c)


def _ssd_mixer(u, w_layers, layer, dt_t, dt_bias, a_log, conv_w, conv_b, dskip, norm_w):
    bsz, seq, d = u.shape
    e = SSD_GROUPS * GROUP_WIDTH
    width = GROUP_WIDTH + 2 * SSD_STATE
    n_proj = GROUP_WIDTH + width
    next_b = lambda b, g: jnp.minimum(b + (g + 1) // SSD_GROUPS, bsz - 1)
    next_g = lambda g: (g + 1) % SSD_GROUPS
    w_specs = lambda grp: [
        pl.BlockSpec((None, d, GROUP_WIDTH), lambda b, g: (layer, 0, grp(g))),
        pl.BlockSpec((None, d, GROUP_WIDTH), lambda b, g: (layer, 0, e // GROUP_WIDTH + grp(g))),
        pl.BlockSpec((None, d, SSD_STATE), lambda b, g: (layer, 0, 2 * e // SSD_STATE + grp(g))),
        pl.BlockSpec((None, d, SSD_STATE),
                     lambda b, g: (layer, 0, 2 * e // SSD_STATE + SSD_GROUPS + grp(g))),
    ]
    in_specs = [pl.BlockSpec((None, seq, d), lambda b, g: (next_b(b, g), 0, 0))]
    in_specs += w_specs(lambda g: 0)
    in_specs += w_specs(next_g)
    in_specs += [
        pl.BlockSpec((8, seq), lambda b, g: (g, b)),
        pl.BlockSpec((8, 1), lambda b, g: (g, 0)),
        pl.BlockSpec((8, 1), lambda b, g: (g, 0)),
        pl.BlockSpec((SSD_CONV, GROUP_WIDTH), lambda b, g: (0, g)),
        pl.BlockSpec((SSD_CONV, SSD_STATE), lambda b, g: (0, e // SSD_STATE + g)),
        pl.BlockSpec((SSD_CONV, SSD_STATE), lambda b, g: (0, e // SSD_STATE + SSD_GROUPS + g)),
        pl.BlockSpec((1, GROUP_WIDTH), lambda b, g: (0, g)),
        pl.BlockSpec((1, SSD_STATE), lambda b, g: (0, e // SSD_STATE + g)),
        pl.BlockSpec((1, SSD_STATE), lambda b, g: (0, e // SSD_STATE + SSD_GROUPS + g)),
        pl.BlockSpec((1, GROUP_WIDTH), lambda b, g: (0, g)),
        pl.BlockSpec((1, GROUP_WIDTH), lambda b, g: (0, g)),
    ]
    selectors = _selectors()
    in_specs += [pl.BlockSpec(s.shape, lambda b, g: (0, 0)) for s in selectors]
    return pl.pallas_call(
        _ssd_kernel, grid=(bsz, SSD_GROUPS), in_specs=in_specs,
        out_specs=pl.BlockSpec((None, seq, GROUP_WIDTH), lambda b, g: (b, 0, g)),
        out_shape=jax.ShapeDtypeStruct((bsz, seq, e), bf16),
        scratch_shapes=[
            pltpu.VMEM((seq + 2 * HALO, width), f32),
            pltpu.VMEM((seq, width), bf16),
            pltpu.VMEM((seq // CHUNK, _STRIP_ROWS, 2 * HEADS_PER_GROUP * CHUNK), bf16),
            pltpu.VMEM((seq, CHUNK), bf16),
            pltpu.VMEM((seq, GROUP_WIDTH), f32),
            pltpu.VMEM((seq // CHUNK, SSD_STATE, 2 * GROUP_WIDTH), f32),
            pltpu.VMEM((seq // CHUNK, 32, 2 * GROUP_WIDTH), f32),
            pltpu.VMEM((seq // CHUNK, SSD_STATE, 2 * GROUP_WIDTH), bf16),
            pltpu.VMEM((seq // CHUNK, CHUNK, 2 * HEADS_PER_GROUP * CHUNK), f32),
            pltpu.VMEM((seq // CHUNK, CHUNK, CHUNK), f32),
            pltpu.VMEM((2, seq, n_proj), bf16),
            pltpu.VMEM((seq, GROUP_WIDTH), bf16),
        ],
        compiler_params=pltpu.CompilerParams(dimension_semantics=("arbitrary", "arbitrary"),
                                             vmem_limit_bytes=SSD_VMEM_LIMIT),
        name="ssd_mixer")(u, *([w_layers] * 8), dt_t, dt_bias, a_log,
                          conv_w, conv_w, conv_w, conv_b, conv_b, conv_b, dskip, norm_w,
                          *selectors)


def _window_sum(win, w):
    half = w // 2
    if w == 2:
        return _rows_ahead(win, HALO - 1) + _rows_ahead(win, HALO)
    t = win + pltpu.roll(win, win.shape[0] - 1, axis=0)
    span = 2
    while span * 2 < w:
        t = t + pltpu.roll(t, win.shape[0] - span, axis=0)
        span *= 2
    return _rows_ahead(t, HALO - half) + _rows_ahead(t, HALO)


def _pool_kernel(v_ref, gate_ref, mix_ref, scale_ref, o_ref, vp_scr):
    seq, gd = v_ref.shape
    n_steps = seq // POOL_ROWS
    gi = pl.program_id(1)

    vp_scr[0:HALO, :] = jnp.zeros((HALO, gd), f32)
    vp_scr[seq + HALO:seq + 2 * HALO, :] = jnp.zeros((HALO, gd), f32)

    def fill(c, carry):
        base = pl.multiple_of(c * POOL_ROWS, POOL_ROWS)
        vp_scr[pl.ds(base + HALO, POOL_ROWS), :] = v_ref[pl.ds(base, POOL_ROWS), :].astype(f32)
        return carry

    lax.fori_loop(0, n_steps, fill, 0)
    scale = scale_ref[...]
    mix = mix_ref[...].astype(bf16)

    for k, w in enumerate(POOL_WINDOWS):
        @pl.when(gi == k)
        def _(w=w):
            half = w // 2

            def step(c, carry):
                base = pl.multiple_of(c * POOL_ROWS, POOL_ROWS)
                win = vp_scr[pl.ds(base, POOL_ROWS + 2 * HALO), :]
                pos = base + lax.broadcasted_iota(jnp.int32, (POOL_ROWS, 1), 0)
                cnt = (jnp.minimum(pos + half, seq) - jnp.maximum(pos - half, 0)).astype(f32)
                pooled = _window_sum(win, w) / cnt - win[HALO:HALO + POOL_ROWS, :]
                mixed = jnp.dot(pooled.astype(bf16), mix, preferred_element_type=f32)
                gate = gate_ref[pl.ds(base, POOL_ROWS), :].astype(f32)
                o_ref[pl.ds(base, POOL_ROWS), :] = (mixed * scale * _silu_of_twice(gate)).astype(o_ref.dtype)
                return carry

            lax.fori_loop(0, n_steps, step, 0)


def _pool_core(proj, mix_layers, layer, scale):
    bsz, seq, e2 = proj.shape
    e = e2 // 2
    ng = len(POOL_WINDOWS)
    gd = e // ng
    return pl.pallas_call(
        _pool_kernel, grid=(bsz, ng),
        in_specs=[pl.BlockSpec((None, seq, gd), lambda b, g: (b, 0, g)),
                  pl.BlockSpec((None, seq, gd), lambda b, g: (b, 0, ng + g)),
                  pl.BlockSpec((None, None, gd, gd), lambda b, g: (layer, g, 0, 0)),
                  pl.BlockSpec((1, gd), lambda b, g: (0, g))],
        out_specs=pl.BlockSpec((None, seq, gd), lambda b, g: (b, 0, g)),
        out_shape=jax.ShapeDtypeStruct((bsz, seq, e), bf16),
        scratch_shapes=[pltpu.VMEM((seq + 2 * HALO, gd), f32)],
        compiler_params=pltpu.CompilerParams(dimension_semantics=("arbitrary", "arbitrary"),
                                             vmem_limit_bytes=VMEM_LIMIT),
        name="pool_core")(proj, proj, mix_layers, scale)


def _group_major(p):
    return p.reshape(2, SSD_GROUPS, HEADS_PER_GROUP).transpose(1, 0, 2).reshape(-1, 1)


def kernel(x, norm_w, ssd_w_in, ssd_conv_w, ssd_conv_b, ssd_dt_bias, ssd_a_log, ssd_d, ssd_norm_w, ssd_w_out, pool_w_in, pool_mix_w, pool_scale, pool_w_out, final_norm_w):
    bsz, seq, d = x.shape
    t = bsz * seq
    depth = norm_w.shape[0]
    e = ssd_w_out.shape[1]
    n_main = 2 * e + 2 * SSD_GROUPS * SSD_STATE
    h = x.reshape(t, d)
    for i in range(depth):
        j = i // 2
        nw = norm_w[i].reshape(1, d)
        final_nw = final_norm_w.reshape(1, d) if i == depth - 1 else None
        if i % 2 == 0:
            w_dt = ssd_w_in[j, :, n_main:]
            wdt_t = (w_dt.reshape(d, 2, SSD_GROUPS, HEADS_PER_GROUP).transpose(2, 1, 3, 0)
                     .reshape(-1, d).astype(bf16))
            u, dt_t = _norm_dt(h, nw, wdt_t)
            y = _ssd_mixer(u.reshape(bsz, seq, d), ssd_w_in, j, dt_t,
                           _group_major(ssd_dt_bias[j]), _group_major(ssd_a_log[j]),
                           0.5 * ssd_conv_w[j], 0.5 * ssd_conv_b[j].reshape(1, -1),
                           jnp.repeat(ssd_d[j], SSD_HEAD_DIM).reshape(1, e),
                           ssd_norm_w[j].reshape(1, e))
            h = _out_proj(y.reshape(t, e), ssd_w_out, j, h, final_nw)
        else:
            proj = _in_proj(h, nw, pool_w_in, j, e)
            y = _pool_core(proj.reshape(bsz, seq, 2 * e), pool_mix_w, j, pool_scale[j].reshape(1, e))
            h = _out_proj(y.reshape(t, e), pool_w_out, j, h, final_nw)
    return h.reshape(bsz, seq, d)
```

```python
import functools

import jax
import jax.numpy as jnp
import numpy as np
from jax import lax
from jax.experimental import pallas as pl
from jax.experimental.pallas import tpu as pltpu

f32 = jnp.float32
bf16 = jnp.bfloat16

EPS = 1e-6
LANES = 128
LOG2E = 1.4426950408889634
SSD_HEAD_DIM = 64
SSD_GROUPS = 8
SSD_STATE = 128
SSD_CONV = 5
HEADS_PER_GROUP = 4
GROUP_WIDTH = HEADS_PER_GROUP * SSD_HEAD_DIM
CHUNK = 128
HALO = 8
POOL_WINDOWS = (2, 4, 8, 16)
POOL_ROWS = 256
POOL_BAND_ROWS = 128
POOL_HALO = 16
PROJ_ROWS = 1024

VMEM_LIMIT = 48 * 1024 * 1024
SSD_VMEM_LIMIT = 58 * 1024 * 1024


def _silu_of_twice(half):
    return half + half * jnp.tanh(half)


def _rows_ahead(win, k):
    n = win.shape[0]
    if k % 8 == 0:
        return win[k:k + n - 2 * HALO, :]
    return pltpu.roll(win, n - k, axis=0)[0:n - 2 * HALO, :]


def _normed(h_ref, nw_ref):
    xv = h_ref[...]
    ms = jnp.mean(xv * xv, axis=-1, keepdims=True)
    return (xv * lax.rsqrt(ms + EPS) * nw_ref[...]).astype(bf16)


def _in_proj_kernel(h_ref, nw_ref, w_ref, o_ref, u_scr):
    @pl.when(pl.program_id(1) == 0)
    def _():
        u_scr[...] = _normed(h_ref, nw_ref)

    o_ref[...] = jnp.dot(u_scr[...], w_ref[...], preferred_element_type=f32).astype(o_ref.dtype)


def _in_proj(h, nw, w, *, tm=1024, tn=1024):
    t, d = h.shape
    n = w.shape[1]
    return pl.pallas_call(
        _in_proj_kernel, grid=(t // tm, n // tn),
        in_specs=[pl.BlockSpec((tm, d), lambda i, j: (i, 0)),
                  pl.BlockSpec((1, d), lambda i, j: (0, 0)),
                  pl.BlockSpec((d, tn), lambda i, j: (0, j))],
        out_specs=pl.BlockSpec((tm, tn), lambda i, j: (i, j)),
        out_shape=jax.ShapeDtypeStruct((t, n), bf16),
        scratch_shapes=[pltpu.VMEM((tm, d), bf16)],
        compiler_params=pltpu.CompilerParams(dimension_semantics=("arbitrary", "arbitrary"),
                                             vmem_limit_bytes=VMEM_LIMIT),
        name="in_proj")(h, nw, w)


def _norm_dt_kernel(h_ref, nw_ref, wdt_ref, u_ref, dt_ref):
    u = _normed(h_ref, nw_ref)
    u_ref[...] = u
    dt_ref[...] = lax.dot_general(wdt_ref[...], u, (((1,), (1,)), ((), ())),
                                  preferred_element_type=f32)


def _norm_dt(h, nw, wdt_t, *, tm=1024):
    t, d = h.shape
    nd = wdt_t.shape[0]
    return pl.pallas_call(
        _norm_dt_kernel, grid=(t // tm,),
        in_specs=[pl.BlockSpec((tm, d), lambda i: (i, 0)),
                  pl.BlockSpec((1, d), lambda i: (0, 0)),
                  pl.BlockSpec((nd, d), lambda i: (0, 0))],
        out_specs=[pl.BlockSpec((tm, d), lambda i: (i, 0)), pl.BlockSpec((nd, tm), lambda i: (0, i))],
        out_shape=[jax.ShapeDtypeStruct((t, d), bf16), jax.ShapeDtypeStruct((nd, t), f32)],
        compiler_params=pltpu.CompilerParams(dimension_semantics=("arbitrary",),
                                             vmem_limit_bytes=VMEM_LIMIT),
        name="norm_dt")(h, nw, wdt_t)


def _out_proj_kernel(y_ref, w_ref, h_ref, o_ref):
    o_ref[...] = h_ref[...] + jnp.dot(y_ref[...], w_ref[...], preferred_element_type=f32)


def _out_proj_final_kernel(y_ref, w_ref, h_ref, nw_ref, o_ref):
    acc = h_ref[...] + jnp.dot(y_ref[...], w_ref[...], preferred_element_type=f32)
    ms = jnp.mean(acc * acc, axis=-1, keepdims=True)
    o_ref[...] = acc * lax.rsqrt(ms + EPS) * nw_ref[...]


def _out_proj(y, w, h, final_nw=None, *, tm=1024):
    t, e = y.shape
    d = w.shape[1]
    in_specs = [pl.BlockSpec((tm, e), lambda i: (i, 0)),
                pl.BlockSpec((e, d), lambda i: (0, 0)),
                pl.BlockSpec((tm, d), lambda i: (i, 0))]
    args = [y, w, h]
    body = _out_proj_kernel
    if final_nw is not None:
        in_specs.append(pl.BlockSpec((1, d), lambda i: (0, 0)))
        args.append(final_nw)
        body = _out_proj_final_kernel
    return pl.pallas_call(
        body, grid=(t // tm,), in_specs=in_specs,
        out_specs=pl.BlockSpec((tm, d), lambda i: (i, 0)),
        out_shape=jax.ShapeDtypeStruct((t, d), f32),
        compiler_params=pltpu.CompilerParams(dimension_semantics=("arbitrary",),
                                             vmem_limit_bytes=VMEM_LIMIT),
        name="out_proj")(*args)


def _split3(v):
    hi = v.astype(bf16).astype(f32)
    rem = v - hi
    mid = rem.astype(bf16).astype(f32)
    return hi, mid, rem - mid


_N_VALS = 24
_SPLIT_ROWS = 3 * _N_VALS
_MASKED = -1e30


_STRIP_BASE = 80
_STRIP_ROWS = 48


def _selectors():
    n_dir = 2 * HEADS_PER_GROUP
    sel_g = np.zeros((_STRIP_BASE, n_dir * CHUNK), np.float32)
    for j in range(n_dir):
        for p in range(3):
            sel_g[p * _N_VALS + j, j * CHUNK:(j + 1) * CHUNK] = 1.0 if j < HEADS_PER_GROUP else -1.0
    sels = [sel_g]
    for fwd_v, bwd_v in ((8, 12), (16, 20)):
        s = np.zeros((CHUNK, 2 * GROUP_WIDTH), np.float32)
        for a, v0 in enumerate((fwd_v, bwd_v)):
            for r in range(HEADS_PER_GROUP):
                lo = a * GROUP_WIDTH + r * SSD_HEAD_DIM
                for p in range(3):
                    s[p * _N_VALS + v0 + r, lo:lo + SSD_HEAD_DIM] = 1.0
        sels.append(s)
    return tuple(jnp.asarray(s, bf16) for s in sels)


def _ssd_kernel(u_ref, wz0_ref, wx0_ref, wb0_ref, wc0_ref, wz_ref, wx_ref, wb_ref, wc_ref,
                dt_ref, dtb_ref, alog_ref,
                cwx_ref, cwb_ref, cwc_ref, cbx_ref, cbb_ref, cbc_ref, dskip_ref, nw_ref,
                selg_ref, sels_ref, selc_ref,
                o_ref, xp_scr, xbc_scr, strip_scr, col_scr, y_scr, s_scr, edge_scr, hin_scr,
                exp_scr, cb_scr, proj_scr, z_scr):
    seq = u_ref.shape[0]
    n_chunks = seq // CHUNK
    width = GROUP_WIDTH + 2 * SSD_STATE
    step = pl.program_id(0) * pl.num_programs(1) + pl.program_id(1)
    cur = step % 2
    nxt = 1 - cur

    @pl.when(step == 0)
    def _():
        w_first = jnp.concatenate([wz0_ref[...], wx0_ref[...], wb0_ref[...], wc0_ref[...]], axis=1)
        for r0 in range(0, seq, PROJ_ROWS):
            proj_scr[0, r0:r0 + PROJ_ROWS, :] = jnp.dot(
                u_ref[r0:r0 + PROJ_ROWS, :], w_first, preferred_element_type=f32).astype(bf16)

    def project_next(q, nt):
        rows = slice(q * PROJ_ROWS, (q + 1) * PROJ_ROWS)
        cols = slice(nt * GROUP_WIDTH, (nt + 1) * GROUP_WIDTH)
        if nt == 0:
            w_tile = wz_ref[...]
        elif nt == 1:
            w_tile = wx_ref[...]
        else:
            w_tile = jnp.concatenate([wb_ref[...], wc_ref[...]], axis=1)
        proj_scr[nxt, rows, cols] = jnp.dot(u_ref[rows, :], w_tile,
                                            preferred_element_type=f32).astype(bf16)

    xp_scr[0:HALO, :] = jnp.zeros((HALO, width), f32)
    xp_scr[seq + HALO:seq + 2 * HALO, :] = jnp.zeros((HALO, width), f32)

    for c in range(n_chunks):
        src_rows = slice(c * CHUNK, (c + 1) * CHUNK)
        rows = slice(c * CHUNK + HALO, (c + 1) * CHUNK + HALO)
        xp_scr[rows, :] = proj_scr[cur, src_rows, GROUP_WIDTH:GROUP_WIDTH + width].astype(f32)
        z_scr[src_rows, :] = proj_scr[cur, src_rows, 0:GROUP_WIDTH]

    n_dir = 2 * HEADS_PER_GROUP
    stack = lambda v: jnp.concatenate([v] * n_chunks, axis=0)
    raw = (jnp.concatenate([dt_ref[:, c * CHUNK:(c + 1) * CHUNK] for c in range(n_chunks)], axis=0)
           + stack(dtb_ref[...]))
    is_fwd = (lax.broadcasted_iota(jnp.int32, raw.shape, 0) & (n_dir - 1)) < HEADS_PER_GROUP
    dt = jnp.maximum(raw, 0.0) + jnp.log1p(jnp.exp(-jnp.abs(raw)))
    a2 = -(dt * stack(jnp.exp(alog_ref[...]) * LOG2E))
    tri = (lax.broadcasted_iota(jnp.int32, (CHUNK, CHUNK), 0)
           <= lax.broadcasted_iota(jnp.int32, (CHUNK, CHUNK), 1)).astype(bf16)
    sums = jnp.dot(jnp.concatenate(_split3(a2), axis=0).astype(bf16), tri, preferred_element_type=f32)
    n_rows = raw.shape[0]
    incl = sums[0:n_rows] + sums[n_rows:2 * n_rows] + sums[2 * n_rows:3 * n_rows]
    tot = incl[:, CHUNK - 1:CHUNK]
    cs = jnp.where(is_fwd, incl, incl - a2)
    log2_dt = jnp.maximum(jnp.log2(dt), -1e30)
    src_parts = _split3(jnp.where(is_fwd, log2_dt - cs, cs + log2_dt))
    to_end = jnp.exp2(tot - cs)
    from_start = jnp.exp2(cs)
    state_w = jnp.where(is_fwd, to_end, from_start) * dt
    carry_w = jnp.where(is_fwd, from_start, to_end)
    val_parts = list(zip(_split3(cs), _split3(state_w), _split3(carry_w)))

    own_block = (lax.broadcasted_iota(jnp.int32, (8, n_dir * CHUNK), 1) // CHUNK
                 == lax.broadcasted_iota(jnp.int32, (8, n_dir * CHUNK), 0))
    eye = (lax.broadcasted_iota(jnp.int32, (CHUNK, CHUNK), 0)
           == lax.broadcasted_iota(jnp.int32, (CHUNK, CHUNK), 1)).astype(bf16)
    ones_rows = jnp.ones((CHUNK - _SPLIT_ROWS, CHUNK), f32)
    zero_rows = jnp.zeros((8, n_dir * CHUNK), f32)
    contract1 = (((1,), (1,)), ((), ()))
    for c in range(n_chunks):
        own = slice(c * n_dir, (c + 1) * n_dir)
        rows = jnp.concatenate([v[own] for part in val_parts for v in part] + [ones_rows],
                               axis=0).astype(bf16)
        col_scr[c * CHUNK:(c + 1) * CHUNK, :] = lax.dot_general(
            eye, rows, contract1, preferred_element_type=f32).astype(bf16)
        strip = []
        for part in src_parts:
            tiled = jnp.concatenate([part[own]] * n_dir, axis=1)
            strip += [jnp.where(own_block, tiled, 0.0), zero_rows]
        strip_scr[c] = jnp.concatenate(strip, axis=0).astype(bf16)
    edges = jnp.concatenate(
        [col_scr[r0:r0 + 16, :] for c in range(n_chunks) for r0 in ((c + 1) * CHUNK - 16, c * CHUNK)],
        axis=0)
    edge_all = jnp.dot(edges, selc_ref[...], preferred_element_type=f32)
    for c in range(n_chunks):
        edge_scr[c] = edge_all[c * 32:(c + 1) * 32]

    conv_w = jnp.concatenate([cwx_ref[...], cwb_ref[...], cwc_ref[...]], axis=1)
    conv_b = jnp.concatenate([cbx_ref[...], cbb_ref[...], cbc_ref[...]], axis=1)
    dskip = dskip_ref[...]
    nw = nw_ref[...]
    lane_head = lax.broadcasted_iota(jnp.int32, (CHUNK, GROUP_WIDTH), 1) // SSD_HEAD_DIM
    contract0 = (((0,), (0,)), ((), ()))

    chunks_per_tile = PROJ_ROWS // CHUNK
    n_col_tiles = (GROUP_WIDTH + width) // GROUP_WIDTH
    tile_at = {}
    for q in range(seq // PROJ_ROWS):
        slots = [(q * chunks_per_tile + i, k) for i in range(chunks_per_tile) for k in (1, 3)]
        for nt in range(n_col_tiles):
            tile_at[slots[nt * len(slots) // n_col_tiles]] = (q, nt)

    def local_pass(c):
        base = c * CHUNK
        win = xp_scr[pl.ds(base, CHUNK + 2 * HALO), :]
        acc = jnp.broadcast_to(conv_b, (CHUNK, width))
        for k in range(SSD_CONV):
            acc = acc + _rows_ahead(win, HALO - SSD_CONV // 2 + k) * conv_w[k:k + 1, :]
            if (c, k) in tile_at:
                project_next(*tile_at[(c, k)])
        xbc = _silu_of_twice(acc)
        xbc_scr[pl.ds(base, CHUNK), :] = xbc.astype(bf16)
        xs = xbc[:, 0:GROUP_WIDTH]
        bm = xbc[:, GROUP_WIDTH:GROUP_WIDTH + SSD_STATE].astype(bf16)
        cm = xbc[:, GROUP_WIDTH + SSD_STATE:width].astype(bf16)
        y_scr[pl.ds(base, CHUNK), :] = xs * dskip
        col = col_scr[pl.ds(base, CHUNK), :]
        exp_scr[c] = jnp.dot(col, jnp.concatenate([selg_ref[...], strip_scr[c]], axis=0),
                             preferred_element_type=f32)
        cb_scr[c] = lax.dot_general(cm, bm, contract1, preferred_element_type=f32)
        state_w = jnp.dot(col, sels_ref[...], preferred_element_type=f32)
        weighted = jnp.concatenate([(xs * state_w[:, 0:GROUP_WIDTH]).astype(bf16),
                                    (xs * state_w[:, GROUP_WIDTH:]).astype(bf16)], axis=1)
        s_scr[c] = lax.dot_general(bm, weighted, contract0, preferred_element_type=f32)

    for c in range(n_chunks):
        local_pass(c)

    def recur(i, carry):
        hf, hb = carry
        cf = i
        cr = n_chunks - 1 - i
        hin_scr[cf, :, 0:GROUP_WIDTH] = hf.astype(bf16)
        hin_scr[cr, :, GROUP_WIDTH:2 * GROUP_WIDTH] = hb.astype(bf16)
        hf = edge_scr[cf][15:16, 0:GROUP_WIDTH] * hf + s_scr[cf][:, 0:GROUP_WIDTH]
        hb = edge_scr[cr][16:17, GROUP_WIDTH:2 * GROUP_WIDTH] * hb + s_scr[cr][:, GROUP_WIDTH:2 * GROUP_WIDTH]
        return hf, hb

    h0 = jnp.zeros((SSD_STATE, GROUP_WIDTH), f32)
    lax.fori_loop(0, n_chunks, recur, (h0, h0))


    li = lax.broadcasted_iota(jnp.int32, (CHUNK, CHUNK), 0)
    si = lax.broadcasted_iota(jnp.int32, (CHUNK, CHUNK), 1)
    lower = si <= li
    upper = si >= li

    def output_pass(c):
        base = c * CHUNK
        xbc = xbc_scr[pl.ds(base, CHUNK), :]
        xs = xbc[:, 0:GROUP_WIDTH]
        cm = xbc[:, GROUP_WIDTH + SSD_STATE:width]
        col = col_scr[pl.ds(base, CHUNK), :]
        carried = (jnp.dot(col, selc_ref[...], preferred_element_type=f32)
                   * jnp.dot(cm, hin_scr[c], preferred_element_type=f32))
        cb = cb_scr[c]
        g_heads, x_heads = [], []
        for r in range(HEADS_PER_GROUP):
            arg_f = exp_scr[c, :, r * CHUNK:(r + 1) * CHUNK]
            arg_b = exp_scr[c, :, (HEADS_PER_GROUP + r) * CHUNK:(HEADS_PER_GROUP + r + 1) * CHUNK]
            decay = (jnp.exp2(jnp.where(lower, arg_f, _MASKED))
                     + jnp.exp2(jnp.where(upper, arg_b, _MASKED)))
            g_heads.append((cb * decay).astype(bf16))
            x_heads.append(jnp.where(lane_head == r, xs, jnp.zeros_like(xs)))
        y = y_scr[pl.ds(base, CHUNK), :] + jnp.dot(
            jnp.concatenate(g_heads, axis=1), jnp.concatenate(x_heads, axis=0),
            preferred_element_type=f32)
        y = y + carried[:, 0:GROUP_WIDTH] + carried[:, GROUP_WIDTH:]
        y = y * _silu_of_twice(z_scr[pl.ds(base, CHUNK), :].astype(f32))
        ms = jnp.mean(y * y, axis=-1, keepdims=True)
        o_ref[pl.ds(base, CHUNK), :] = (y * lax.rsqrt(ms + EPS) * nw).astype(o_ref.dtype)

    for c in range(n_chunks):
        output_pass(c)


def _ssd_mixer(u, w_main, dt_t, dt_bias, a_log, conv_w, conv_b, dskip, norm_w):
    bsz, seq, d = u.shape
    e = SSD_GROUPS * GROUP_WIDTH
    width = GROUP_WIDTH + 2 * SSD_STATE
    n_proj = GROUP_WIDTH + width
    next_b = lambda b, g: jnp.minimum(b + (g + 1) // SSD_GROUPS, bsz - 1)
    next_g = lambda g: (g + 1) % SSD_GROUPS
    w_specs = lambda grp: [
        pl.BlockSpec((d, GROUP_WIDTH), lambda b, g: (0, grp(g))),
        pl.BlockSpec((d, GROUP_WIDTH), lambda b, g: (0, e // GROUP_WIDTH + grp(g))),
        pl.BlockSpec((d, SSD_STATE), lambda b, g: (0, 2 * e // SSD_STATE + grp(g))),
        pl.BlockSpec((d, SSD_STATE), lambda b, g: (0, 2 * e // SSD_STATE + SSD_GROUPS + grp(g))),
    ]
    in_specs = [pl.BlockSpec((None, seq, d), lambda b, g: (next_b(b, g), 0, 0))]
    in_specs += w_specs(lambda g: 0)
    in_specs += w_specs(next_g)
    in_specs += [
        pl.BlockSpec((8, seq), lambda b, g: (g, b)),
        pl.BlockSpec((8, 1), lambda b, g: (g, 0)),
        pl.BlockSpec((8, 1), lambda b, g: (g, 0)),
        pl.BlockSpec((SSD_CONV, GROUP_WIDTH), lambda b, g: (0, g)),
        pl.BlockSpec((SSD_CONV, SSD_STATE), lambda b, g: (0, e // SSD_STATE + g)),
        pl.BlockSpec((SSD_CONV, SSD_STATE), lambda b, g: (0, e // SSD_STATE + SSD_GROUPS + g)),
        pl.BlockSpec((1, GROUP_WIDTH), lambda b, g: (0, g)),
        pl.BlockSpec((1, SSD_STATE), lambda b, g: (0, e // SSD_STATE + g)),
        pl.BlockSpec((1, SSD_STATE), lambda b, g: (0, e // SSD_STATE + SSD_GROUPS + g)),
        pl.BlockSpec((1, GROUP_WIDTH), lambda b, g: (0, g)),
        pl.BlockSpec((1, GROUP_WIDTH), lambda b, g: (0, g)),
    ]
    selectors = _selectors()
    in_specs += [pl.BlockSpec(s.shape, lambda b, g: (0, 0)) for s in selectors]
    return pl.pallas_call(
        _ssd_kernel, grid=(bsz, SSD_GROUPS), in_specs=in_specs,
        out_specs=pl.BlockSpec((None, seq, GROUP_WIDTH), lambda b, g: (b, 0, g)),
        out_shape=jax.ShapeDtypeStruct((bsz, seq, e), bf16),
        scratch_shapes=[
            pltpu.VMEM((seq + 2 * HALO, width), f32),
            pltpu.VMEM((seq, width), bf16),
            pltpu.VMEM((seq // CHUNK, _STRIP_ROWS, 2 * HEADS_PER_GROUP * CHUNK), bf16),
            pltpu.VMEM((seq, CHUNK), bf16),
            pltpu.VMEM((seq, GROUP_WIDTH), f32),
            pltpu.VMEM((seq // CHUNK, SSD_STATE, 2 * GROUP_WIDTH), f32),
            pltpu.VMEM((seq // CHUNK, 32, 2 * GROUP_WIDTH), f32),
            pltpu.VMEM((seq // CHUNK, SSD_STATE, 2 * GROUP_WIDTH), bf16),
            pltpu.VMEM((seq // CHUNK, CHUNK, 2 * HEADS_PER_GROUP * CHUNK), f32),
            pltpu.VMEM((seq // CHUNK, CHUNK, CHUNK), f32),
            pltpu.VMEM((2, seq, n_proj), bf16),
            pltpu.VMEM((seq, GROUP_WIDTH), bf16),
        ],
        compiler_params=pltpu.CompilerParams(dimension_semantics=("arbitrary", "arbitrary"),
                                             vmem_limit_bytes=SSD_VMEM_LIMIT),
        name="ssd_mixer")(u, *([w_main] * 8), dt_t, dt_bias, a_log,
                          conv_w, conv_w, conv_w, conv_b, conv_b, conv_b, dskip, norm_w,
                          *selectors)


def _pool_bands():
    m = np.arange(POOL_BAND_ROWS)[:, None] + POOL_HALO
    j = np.arange(POOL_BAND_ROWS + 2 * POOL_HALO)[None, :]
    bands = [((j >= m - w // 2) & (j < m + w // 2)).astype(np.float32) for w in POOL_WINDOWS]
    return jnp.asarray(np.stack(bands), bf16)


def _pool_kernel(half_ref, v_ref, gate_ref, band_ref, mix_ref, scale_ref, o_ref, vp_scr):
    seq, gd = v_ref.shape
    n_steps = seq // POOL_ROWS
    half = half_ref[pl.program_id(1)]

    vp_scr[0:POOL_HALO, :] = jnp.zeros((POOL_HALO, gd), bf16)
    vp_scr[seq + POOL_HALO:seq + 2 * POOL_HALO, :] = jnp.zeros((POOL_HALO, gd), bf16)

    def fill(c, carry):
        base = pl.multiple_of(c * POOL_ROWS, POOL_ROWS)
        vp_scr[pl.ds(base + POOL_HALO, POOL_ROWS), :] = v_ref[pl.ds(base, POOL_ROWS), :]
        return carry

    lax.fori_loop(0, n_steps, fill, 0)
    scale = scale_ref[...]

    def step(c, carry):
        base = pl.multiple_of(c * POOL_ROWS, POOL_ROWS)
        sums, centre = [], []
        for sub in range(0, POOL_ROWS, POOL_BAND_ROWS):
            win = vp_scr[pl.ds(base + sub, POOL_BAND_ROWS + 2 * POOL_HALO), :]
            sums.append(jnp.dot(band_ref[...], win, preferred_element_type=f32))
            centre.append(win[POOL_HALO:POOL_HALO + POOL_BAND_ROWS, :].astype(f32))
        pos = base + lax.broadcasted_iota(jnp.int32, (POOL_ROWS, 1), 0)
        cnt = (jnp.minimum(pos + half, seq) - jnp.maximum(pos - half, 0)).astype(f32)
        pooled = jnp.concatenate(sums, axis=0) / cnt - jnp.concatenate(centre, axis=0)
        mixed = jnp.dot(pooled.astype(bf16), mix_ref[...], preferred_element_type=f32)
        gate = gate_ref[pl.ds(base, POOL_ROWS), :].astype(f32)
        o_ref[pl.ds(base, POOL_ROWS), :] = (mixed * scale * _silu_of_twice(gate)).astype(o_ref.dtype)
        return carry

    lax.fori_loop(0, n_steps, step, 0, unroll=True)


def _pool_core(proj, mix_w, scale):
    bsz, seq, e2 = proj.shape
    e = e2 // 2
    ng = len(POOL_WINDOWS)
    gd = e // ng
    bands = _pool_bands()
    halves = jnp.asarray([w // 2 for w in POOL_WINDOWS], jnp.int32)
    grid_spec = pltpu.PrefetchScalarGridSpec(
        num_scalar_prefetch=1, grid=(bsz, ng),
        in_specs=[pl.BlockSpec((None, seq, gd), lambda b, g, hs: (b, 0, g)),
                  pl.BlockSpec((None, seq, gd), lambda b, g, hs: (b, 0, ng + g)),
                  pl.BlockSpec((None,) + bands.shape[1:], lambda b, g, hs: (g, 0, 0)),
                  pl.BlockSpec((None, gd, gd), lambda b, g, hs: (g, 0, 0)),
                  pl.BlockSpec((1, gd), lambda b, g, hs: (0, g))],
        out_specs=pl.BlockSpec((None, seq, gd), lambda b, g, hs: (b, 0, g)),
        scratch_shapes=[pltpu.VMEM((seq + 2 * POOL_HALO, gd), bf16)])
    return pl.pallas_call(
        _pool_kernel, grid_spec=grid_spec,
        out_shape=jax.ShapeDtypeStruct((bsz, seq, e), bf16),
        compiler_params=pltpu.CompilerParams(dimension_semantics=("arbitrary", "arbitrary"),
                                             vmem_limit_bytes=VMEM_LIMIT),
        name="pool_core")(halves, proj, proj, bands, mix_w, scale)


def _group_major(p):
    return p.reshape(2, SSD_GROUPS, HEADS_PER_GROUP).transpose(1, 0, 2).reshape(-1, 1)


def kernel(x, norm_w, ssd_w_in, ssd_conv_w, ssd_conv_b, ssd_dt_bias, ssd_a_log, ssd_d, ssd_norm_w, ssd_w_out, pool_w_in, pool_mix_w, pool_scale, pool_w_out, final_norm_w):
    bsz, seq, d = x.shape
    t = bsz * seq
    depth = norm_w.shape[0]
    e = ssd_w_out.shape[1]
    n_main = 2 * e + 2 * SSD_GROUPS * SSD_STATE
    h = x.reshape(t, d)
    for i in range(depth):
        j = i // 2
        nw = norm_w[i].reshape(1, d)
        final_nw = final_norm_w.reshape(1, d) if i == depth - 1 else None
        if i % 2 == 0:
            w = ssd_w_in[j]
            z_half = jnp.where(jnp.arange(n_main) < e, 0.5, 1.0).astype(f32)
            w_main = (w[:, :n_main] * z_half).astype(bf16)
            w_dt = w[:, n_main:]
            wdt_t = (w_dt.reshape(d, 2, SSD_GROUPS, HEADS_PER_GROUP).transpose(2, 1, 3, 0)
                     .reshape(-1, d).astype(bf16))
            u, dt_t = _norm_dt(h, nw, wdt_t)
            y = _ssd_mixer(u.reshape(bsz, seq, d), w_main, dt_t,
                           _group_major(ssd_dt_bias[j]), _group_major(ssd_a_log[j]),
                           0.5 * ssd_conv_w[j], 0.5 * ssd_conv_b[j].reshape(1, -1),
                           jnp.repeat(ssd_d[j], SSD_HEAD_DIM).reshape(1, e),
                           ssd_norm_w[j].reshape(1, e))
            h = _out_proj(y.reshape(t, e), ssd_w_out[j].astype(bf16), h, final_nw)
        else:
            gate_half = jnp.where(jnp.arange(2 * e) < e, 1.0, 0.5).astype(f32)
            proj = _in_proj(h, nw, (pool_w_in[j] * gate_half).astype(bf16))
            y = _pool_core(proj.reshape(bsz, seq, 2 * e), pool_mix_w[j].astype(bf16),
                           pool_scale[j].reshape(1, e))
            h = _out_proj(y.reshape(t, e), pool_w_out[j].astype(bf16), h, final_nw)
    return h.reshape(bsz, seq, d)
```

```python
import functools

import jax
import jax.numpy as jnp
import numpy as np
from jax import lax
from jax.experimental import pallas as pl
from jax.experimental.pallas import tpu as pltpu

f32 = jnp.float32
bf16 = jnp.bfloat16

EPS = 1e-6
LANES = 128
LOG2E = 1.4426950408889634
SSD_HEAD_DIM = 64
SSD_GROUPS = 8
SSD_STATE = 128
SSD_CONV = 5
HEADS_PER_GROUP = 4
GROUP_WIDTH = HEADS_PER_GROUP * SSD_HEAD_DIM
CHUNK = 128
HALO = 8
POOL_WINDOWS = (2, 4, 8, 16)
POOL_ROWS = 256
POOL_BAND_ROWS = 128
POOL_HALO = 16
PROJ_ROWS = 1024

VMEM_LIMIT = 48 * 1024 * 1024
SSD_VMEM_LIMIT = 58 * 1024 * 1024


def _silu_of_twice(half):
    return half + half * jnp.tanh(half)


def _rows_ahead(win, k):
    n = win.shape[0]
    if k % 8 == 0:
        return win[k:k + n - 2 * HALO, :]
    return pltpu.roll(win, n - k, axis=0)[0:n - 2 * HALO, :]


def _normed(h_ref, nw_ref):
    xv = h_ref[...]
    ms = jnp.mean(xv * xv, axis=-1, keepdims=True)
    return (xv * lax.rsqrt(ms + EPS) * nw_ref[...]).astype(bf16)


def _in_proj_kernel(h_ref, nw_ref, w_ref, o_ref, u_scr):
    @pl.when(pl.program_id(1) == 0)
    def _():
        u_scr[...] = _normed(h_ref, nw_ref)

    o_ref[...] = jnp.dot(u_scr[...], w_ref[...], preferred_element_type=f32).astype(o_ref.dtype)


def _in_proj(h, nw, w, *, tm=1024, tn=1024):
    t, d = h.shape
    n = w.shape[1]
    return pl.pallas_call(
        _in_proj_kernel, grid=(t // tm, n // tn),
        in_specs=[pl.BlockSpec((tm, d), lambda i, j: (i, 0)),
                  pl.BlockSpec((1, d), lambda i, j: (0, 0)),
                  pl.BlockSpec((d, tn), lambda i, j: (0, j))],
        out_specs=pl.BlockSpec((tm, tn), lambda i, j: (i, j)),
        out_shape=jax.ShapeDtypeStruct((t, n), bf16),
        scratch_shapes=[pltpu.VMEM((tm, d), bf16)],
        compiler_params=pltpu.CompilerParams(dimension_semantics=("arbitrary", "arbitrary"),
                                             vmem_limit_bytes=VMEM_LIMIT),
        name="in_proj")(h, nw, w)


def _norm_dt_kernel(h_ref, nw_ref, wdt_ref, u_ref, dt_ref):
    u = _normed(h_ref, nw_ref)
    u_ref[...] = u
    dt_ref[...] = lax.dot_general(wdt_ref[...], u, (((1,), (1,)), ((), ())),
                                  preferred_element_type=f32)


def _norm_dt(h, nw, wdt_t, *, tm=1024):
    t, d = h.shape
    nd = wdt_t.shape[0]
    return pl.pallas_call(
        _norm_dt_kernel, grid=(t // tm,),
        in_specs=[pl.BlockSpec((tm, d), lambda i: (i, 0)),
                  pl.BlockSpec((1, d), lambda i: (0, 0)),
                  pl.BlockSpec((nd, d), lambda i: (0, 0))],
        out_specs=[pl.BlockSpec((tm, d), lambda i: (i, 0)), pl.BlockSpec((nd, tm), lambda i: (0, i))],
        out_shape=[jax.ShapeDtypeStruct((t, d), bf16), jax.ShapeDtypeStruct((nd, t), f32)],
        compiler_params=pltpu.CompilerParams(dimension_semantics=("arbitrary",),
                                             vmem_limit_bytes=VMEM_LIMIT),
        name="norm_dt")(h, nw, wdt_t)


def _out_proj_kernel(y_ref, w_ref, h_ref, o_ref):
    o_ref[...] = h_ref[...] + jnp.dot(y_ref[...], w_ref[...], preferred_element_type=f32)


def _out_proj_final_kernel(y_ref, w_ref, h_ref, nw_ref, o_ref):
    acc = h_ref[...] + jnp.dot(y_ref[...], w_ref[...], preferred_element_type=f32)
    ms = jnp.mean(acc * acc, axis=-1, keepdims=True)
    o_ref[...] = acc * lax.rsqrt(ms + EPS) * nw_ref[...]


def _out_proj(y, w, h, final_nw=None, *, tm=1024):
    t, e = y.shape
    d = w.shape[1]
    in_specs = [pl.BlockSpec((tm, e), lambda i: (i, 0)),
                pl.BlockSpec((e, d), lambda i: (0, 0)),
                pl.BlockSpec((tm, d), lambda i: (i, 0))]
    args = [y, w, h]
    body = _out_proj_kernel
    if final_nw is not None:
        in_specs.append(pl.BlockSpec((1, d), lambda i: (0, 0)))
        args.append(final_nw)
        body = _out_proj_final_kernel
    return pl.pallas_call(
        body, grid=(t // tm,), in_specs=in_specs,
        out_specs=pl.BlockSpec((tm, d), lambda i: (i, 0)),
        out_shape=jax.ShapeDtypeStruct((t, d), f32),
        compiler_params=pltpu.CompilerParams(dimension_semantics=("arbitrary",),
                                             vmem_limit_bytes=VMEM_LIMIT),
        name="out_proj")(*args)


def _split3(v):
    hi = v.astype(bf16).astype(f32)
    rem = v - hi
    mid = rem.astype(bf16).astype(f32)
    return hi, mid, rem - mid


_N_VALS = 24
_SPLIT_ROWS = 3 * _N_VALS
_MASKED = -1e30


_STRIP_BASE = 80
_STRIP_ROWS = 48


def _selectors():
    n_dir = 2 * HEADS_PER_GROUP
    sel_g = np.zeros((_STRIP_BASE, n_dir * CHUNK), np.float32)
    for j in range(n_dir):
        for p in range(3):
            sel_g[p * _N_VALS + j, j * CHUNK:(j + 1) * CHUNK] = 1.0 if j < HEADS_PER_GROUP else -1.0
    sels = [sel_g]
    for fwd_v, bwd_v in ((8, 12), (16, 20)):
        s = np.zeros((CHUNK, 2 * GROUP_WIDTH), np.float32)
        for a, v0 in enumerate((fwd_v, bwd_v)):
            for r in range(HEADS_PER_GROUP):
                lo = a * GROUP_WIDTH + r * SSD_HEAD_DIM
                for p in range(3):
                    s[p * _N_VALS + v0 + r, lo:lo + SSD_HEAD_DIM] = 1.0
        sels.append(s)
    return tuple(jnp.asarray(s, bf16) for s in sels)


def _ssd_kernel(u_ref, wz0_ref, wx0_ref, wb0_ref, wc0_ref, wz_ref, wx_ref, wb_ref, wc_ref,
                dt_ref, dtb_ref, alog_ref,
                cwx_ref, cwb_ref, cwc_ref, cbx_ref, cbb_ref, cbc_ref, dskip_ref, nw_ref,
                selg_ref, sels_ref, selc_ref,
                o_ref, xp_scr, xbc_scr, strip_scr, col_scr, y_scr, s_scr, edge_scr, hin_scr,
                exp_scr, cb_scr, proj_scr, gate_scr):
    seq = u_ref.shape[0]
    n_chunks = seq // CHUNK
    width = GROUP_WIDTH + 2 * SSD_STATE
    step = pl.program_id(0) * pl.num_programs(1) + pl.program_id(1)
    cur = step % 2
    nxt = 1 - cur

    @pl.when(step == 0)
    def _():
        w_first = jnp.concatenate([wz0_ref[...], wx0_ref[...], wb0_ref[...], wc0_ref[...]], axis=1)
        for r0 in range(0, seq, PROJ_ROWS):
            proj_scr[0, r0:r0 + PROJ_ROWS, :] = jnp.dot(
                u_ref[r0:r0 + PROJ_ROWS, :], w_first, preferred_element_type=f32).astype(bf16)

    def project_next(q, nt):
        rows = slice(q * PROJ_ROWS, (q + 1) * PROJ_ROWS)
        cols = slice(nt * GROUP_WIDTH, (nt + 1) * GROUP_WIDTH)
        if nt == 0:
            w_tile = wz_ref[...]
        elif nt == 1:
            w_tile = wx_ref[...]
        else:
            w_tile = jnp.concatenate([wb_ref[...], wc_ref[...]], axis=1)
        proj_scr[nxt, rows, cols] = jnp.dot(u_ref[rows, :], w_tile,
                                            preferred_element_type=f32).astype(bf16)

    xp_scr[0:HALO, :] = jnp.zeros((HALO, width), f32)
    xp_scr[seq + HALO:seq + 2 * HALO, :] = jnp.zeros((HALO, width), f32)

    for c in range(n_chunks):
        src_rows = slice(c * CHUNK, (c + 1) * CHUNK)
        rows = slice(c * CHUNK + HALO, (c + 1) * CHUNK + HALO)
        xp_scr[rows, :] = proj_scr[cur, src_rows, GROUP_WIDTH:GROUP_WIDTH + width].astype(f32)
        gate_scr[src_rows, :] = _silu_of_twice(proj_scr[cur, src_rows, 0:GROUP_WIDTH].astype(f32))

    n_dir = 2 * HEADS_PER_GROUP
    stack = lambda v: jnp.concatenate([v] * n_chunks, axis=0)
    raw = (jnp.concatenate([dt_ref[:, c * CHUNK:(c + 1) * CHUNK] for c in range(n_chunks)], axis=0)
           + stack(dtb_ref[...]))
    is_fwd = (lax.broadcasted_iota(jnp.int32, raw.shape, 0) & (n_dir - 1)) < HEADS_PER_GROUP
    dt = jnp.maximum(raw, 0.0) + jnp.log1p(jnp.exp(-jnp.abs(raw)))
    a2 = -(dt * stack(jnp.exp(alog_ref[...]) * LOG2E))
    tri = (lax.broadcasted_iota(jnp.int32, (CHUNK, CHUNK), 0)
           <= lax.broadcasted_iota(jnp.int32, (CHUNK, CHUNK), 1)).astype(bf16)
    sums = jnp.dot(jnp.concatenate(_split3(a2), axis=0).astype(bf16), tri, preferred_element_type=f32)
    n_rows = raw.shape[0]
    incl = sums[0:n_rows] + sums[n_rows:2 * n_rows] + sums[2 * n_rows:3 * n_rows]
    tot = incl[:, CHUNK - 1:CHUNK]
    cs = jnp.where(is_fwd, incl, incl - a2)
    log2_dt = jnp.maximum(jnp.log2(dt), -1e30)
    src_parts = _split3(jnp.where(is_fwd, log2_dt - cs, cs + log2_dt))
    to_end = jnp.exp2(tot - cs)
    from_start = jnp.exp2(cs)
    state_w = jnp.where(is_fwd, to_end, from_start) * dt
    carry_w = jnp.where(is_fwd, from_start, to_end)
    val_parts = list(zip(_split3(cs), _split3(state_w), _split3(carry_w)))

    own_block = (lax.broadcasted_iota(jnp.int32, (8, n_dir * CHUNK), 1) // CHUNK
                 == lax.broadcasted_iota(jnp.int32, (8, n_dir * CHUNK), 0))
    eye = (lax.broadcasted_iota(jnp.int32, (CHUNK, CHUNK), 0)
           == lax.broadcasted_iota(jnp.int32, (CHUNK, CHUNK), 1)).astype(bf16)
    ones_rows = jnp.ones((CHUNK - _SPLIT_ROWS, CHUNK), f32)
    zero_rows = jnp.zeros((8, n_dir * CHUNK), f32)
    contract1 = (((1,), (1,)), ((), ()))
    for c in range(n_chunks):
        own = slice(c * n_dir, (c + 1) * n_dir)
        rows = jnp.concatenate([v[own] for part in val_parts for v in part] + [ones_rows],
                               axis=0).astype(bf16)
        col_scr[c * CHUNK:(c + 1) * CHUNK, :] = lax.dot_general(
            eye, rows, contract1, preferred_element_type=f32).astype(bf16)
        strip = []
        for part in src_parts:
            tiled = jnp.concatenate([part[own]] * n_dir, axis=1)
            strip += [jnp.where(own_block, tiled, 0.0), zero_rows]
        strip_scr[c] = jnp.concatenate(strip, axis=0).astype(bf16)
    edges = jnp.concatenate(
        [col_scr[r0:r0 + 16, :] for c in range(n_chunks) for r0 in ((c + 1) * CHUNK - 16, c * CHUNK)],
        axis=0)
    edge_all = jnp.dot(edges, selc_ref[...], preferred_element_type=f32)
    for c in range(n_chunks):
        edge_scr[c] = edge_all[c * 32:(c + 1) * 32]

    conv_w = jnp.concatenate([cwx_ref[...], cwb_ref[...], cwc_ref[...]], axis=1)
    conv_b = jnp.concatenate([cbx_ref[...], cbb_ref[...], cbc_ref[...]], axis=1)
    dskip = dskip_ref[...]
    nw = nw_ref[...]
    lane_head = lax.broadcasted_iota(jnp.int32, (CHUNK, GROUP_WIDTH), 1) // SSD_HEAD_DIM
    contract0 = (((0,), (0,)), ((), ()))

    chunks_per_tile = PROJ_ROWS // CHUNK
    n_col_tiles = (GROUP_WIDTH + width) // GROUP_WIDTH
    tile_at = {}
    for q in range(seq // PROJ_ROWS):
        slots = [(q * chunks_per_tile + i, k) for i in range(chunks_per_tile) for k in (1, 3)]
        for nt in range(n_col_tiles):
            tile_at[slots[nt * len(slots) // n_col_tiles]] = (q, nt)

    def local_pass(c):
        base = c * CHUNK
        win = xp_scr[pl.ds(base, CHUNK + 2 * HALO), :]
        acc = jnp.broadcast_to(conv_b, (CHUNK, width))
        for k in range(SSD_CONV):
            acc = acc + _rows_ahead(win, HALO - SSD_CONV // 2 + k) * conv_w[k:k + 1, :]
            if (c, k) in tile_at:
                project_next(*tile_at[(c, k)])
        xbc = _silu_of_twice(acc)
        xbc_scr[pl.ds(base, CHUNK), :] = xbc.astype(bf16)
        xs = xbc[:, 0:GROUP_WIDTH]
        bm = xbc[:, GROUP_WIDTH:GROUP_WIDTH + SSD_STATE].astype(bf16)
        cm = xbc[:, GROUP_WIDTH + SSD_STATE:width].astype(bf16)
        y_scr[pl.ds(base, CHUNK), :] = xs * dskip
        col = col_scr[pl.ds(base, CHUNK), :]
        exp_scr[c] = jnp.dot(col, jnp.concatenate([selg_ref[...], strip_scr[c]], axis=0),
                             preferred_element_type=f32)
        cb_scr[c] = lax.dot_general(cm, bm, contract1, preferred_element_type=f32)
        state_w = jnp.dot(col, sels_ref[...], preferred_element_type=f32)
        weighted = jnp.concatenate([(xs * state_w[:, 0:GROUP_WIDTH]).astype(bf16),
                                    (xs * state_w[:, GROUP_WIDTH:]).astype(bf16)], axis=1)
        s_scr[c] = lax.dot_general(bm, weighted, contract0, preferred_element_type=f32)

    for c in range(n_chunks):
        local_pass(c)

    def recur(i, carry):
        hf, hb = carry
        cf = i
        cr = n_chunks - 1 - i
        hin_scr[cf, :, 0:GROUP_WIDTH] = hf.astype(bf16)
        hin_scr[cr, :, GROUP_WIDTH:2 * GROUP_WIDTH] = hb.astype(bf16)
        hf = edge_scr[cf][15:16, 0:GROUP_WIDTH] * hf + s_scr[cf][:, 0:GROUP_WIDTH]
        hb = edge_scr[cr][16:17, GROUP_WIDTH:2 * GROUP_WIDTH] * hb + s_scr[cr][:, GROUP_WIDTH:2 * GROUP_WIDTH]
        return hf, hb

    h0 = jnp.zeros((SSD_STATE, GROUP_WIDTH), f32)
    lax.fori_loop(0, n_chunks, recur, (h0, h0))


    li = lax.broadcasted_iota(jnp.int32, (CHUNK, CHUNK), 0)
    si = lax.broadcasted_iota(jnp.int32, (CHUNK, CHUNK), 1)
    lower = si <= li
    upper = si >= li

    def output_pass(c):
        base = c * CHUNK
        xbc = xbc_scr[pl.ds(base, CHUNK), :]
        xs = xbc[:, 0:GROUP_WIDTH]
        cm = xbc[:, GROUP_WIDTH + SSD_STATE:width]
        col = col_scr[pl.ds(base, CHUNK), :]
        carried = (jnp.dot(col, selc_ref[...], preferred_element_type=f32)
                   * jnp.dot(cm, hin_scr[c], preferred_element_type=f32))
        cb = cb_scr[c]
        g_heads, x_heads = [], []
        for r in range(HEADS_PER_GROUP):
            arg_f = exp_scr[c, :, r * CHUNK:(r + 1) * CHUNK]
            arg_b = exp_scr[c, :, (HEADS_PER_GROUP + r) * CHUNK:(HEADS_PER_GROUP + r + 1) * CHUNK]
            decay = (jnp.exp2(jnp.where(lower, arg_f, _MASKED))
                     + jnp.exp2(jnp.where(upper, arg_b, _MASKED)))
            g_heads.append((cb * decay).astype(bf16))
            x_heads.append(jnp.where(lane_head == r, xs, jnp.zeros_like(xs)))
        y = y_scr[pl.ds(base, CHUNK), :] + jnp.dot(
            jnp.concatenate(g_heads, axis=1), jnp.concatenate(x_heads, axis=0),
            preferred_element_type=f32)
        y = y + carried[:, 0:GROUP_WIDTH] + carried[:, GROUP_WIDTH:]
        y = y * gate_scr[pl.ds(base, CHUNK), :]
        ms = jnp.mean(y * y, axis=-1, keepdims=True)
        o_ref[pl.ds(base, CHUNK), :] = (y * lax.rsqrt(ms + EPS) * nw).astype(o_ref.dtype)

    for c in range(n_chunks):
        output_pass(c)


def _ssd_mixer(u, w_main, dt_t, dt_bias, a_log, conv_w, conv_b, dskip, norm_w):
    bsz, seq, d = u.shape
    e = SSD_GROUPS * GROUP_WIDTH
    width = GROUP_WIDTH + 2 * SSD_STATE
    n_proj = GROUP_WIDTH + width
    next_b = lambda b, g: jnp.minimum(b + (g + 1) // SSD_GROUPS, bsz - 1)
    next_g = lambda g: (g + 1) % SSD_GROUPS
    w_specs = lambda grp: [
        pl.BlockSpec((d, GROUP_WIDTH), lambda b, g: (0, grp(g))),
        pl.BlockSpec((d, GROUP_WIDTH), lambda b, g: (0, e // GROUP_WIDTH + grp(g))),
        pl.BlockSpec((d, SSD_STATE), lambda b, g: (0, 2 * e // SSD_STATE + grp(g))),
        pl.BlockSpec((d, SSD_STATE), lambda b, g: (0, 2 * e // SSD_STATE + SSD_GROUPS + grp(g))),
    ]
    in_specs = [pl.BlockSpec((None, seq, d), lambda b, g: (next_b(b, g), 0, 0))]
    in_specs += w_specs(lambda g: 0)
    in_specs += w_specs(next_g)
    in_specs += [
        pl.BlockSpec((8, seq), lambda b, g: (g, b)),
        pl.BlockSpec((8, 1), lambda b, g: (g, 0)),
        pl.BlockSpec((8, 1), lambda b, g: (g, 0)),
        pl.BlockSpec((SSD_CONV, GROUP_WIDTH), lambda b, g: (0, g)),
        pl.BlockSpec((SSD_CONV, SSD_STATE), lambda b, g: (0, e // SSD_STATE + g)),
        pl.BlockSpec((SSD_CONV, SSD_STATE), lambda b, g: (0, e // SSD_STATE + SSD_GROUPS + g)),
        pl.BlockSpec((1, GROUP_WIDTH), lambda b, g: (0, g)),
        pl.BlockSpec((1, SSD_STATE), lambda b, g: (0, e // SSD_STATE + g)),
        pl.BlockSpec((1, SSD_STATE), lambda b, g: (0, e // SSD_STATE + SSD_GROUPS + g)),
        pl.BlockSpec((1, GROUP_WIDTH), lambda b, g: (0, g)),
        pl.BlockSpec((1, GROUP_WIDTH), lambda b, g: (0, g)),
    ]
    selectors = _selectors()
    in_specs += [pl.BlockSpec(s.shape, lambda b, g: (0, 0)) for s in selectors]
    return pl.pallas_call(
        _ssd_kernel, grid=(bsz, SSD_GROUPS), in_specs=in_specs,
        out_specs=pl.BlockSpec((None, seq, GROUP_WIDTH), lambda b, g: (b, 0, g)),
        out_shape=jax.ShapeDtypeStruct((bsz, seq, e), bf16),
        scratch_shapes=[
            pltpu.VMEM((seq + 2 * HALO, width), f32),
            pltpu.VMEM((seq, width), bf16),
            pltpu.VMEM((seq // CHUNK, _STRIP_ROWS, 2 * HEADS_PER_GROUP * CHUNK), bf16),
            pltpu.VMEM((seq, CHUNK), bf16),
            pltpu.VMEM((seq, GROUP_WIDTH), f32),
            pltpu.VMEM((seq // CHUNK, SSD_STATE, 2 * GROUP_WIDTH), f32),
            pltpu.VMEM((seq // CHUNK, 32, 2 * GROUP_WIDTH), f32),
            pltpu.VMEM((seq // CHUNK, SSD_STATE, 2 * GROUP_WIDTH), bf16),
            pltpu.VMEM((seq // CHUNK, CHUNK, 2 * HEADS_PER_GROUP * CHUNK), f32),
            pltpu.VMEM((seq // CHUNK, CHUNK, CHUNK), f32),
            pltpu.VMEM((2, seq, n_proj), bf16),
            pltpu.VMEM((seq, GROUP_WIDTH), f32),
        ],
        compiler_params=pltpu.CompilerParams(dimension_semantics=("arbitrary", "arbitrary"),
                                             vmem_limit_bytes=SSD_VMEM_LIMIT),
        name="ssd_mixer")(u, *([w_main] * 8), dt_t, dt_bias, a_log,
                          conv_w, conv_w, conv_w, conv_b, conv_b, conv_b, dskip, norm_w,
                          *selectors)


def _pool_bands():
    m = np.arange(POOL_BAND_ROWS)[:, None] + POOL_HALO
    j = np.arange(POOL_BAND_ROWS + 2 * POOL_HALO)[None, :]
    bands = [((j >= m - w // 2) & (j < m + w // 2)).astype(np.float32) for w in POOL_WINDOWS]
    return jnp.asarray(np.stack(bands), bf16)


def _pool_kernel(half_ref, v_ref, gate_ref, band_ref, mix_ref, scale_ref, o_ref, vp_scr):
    seq, gd = v_ref.shape
    n_steps = seq // POOL_ROWS
    half = half_ref[pl.program_id(1)]

    vp_scr[0:POOL_HALO, :] = jnp.zeros((POOL_HALO, gd), bf16)
    vp_scr[seq + POOL_HALO:seq + 2 * POOL_HALO, :] = jnp.zeros((POOL_HALO, gd), bf16)

    def fill(c, carry):
        base = pl.multiple_of(c * POOL_ROWS, POOL_ROWS)
        vp_scr[pl.ds(base + POOL_HALO, POOL_ROWS), :] = v_ref[pl.ds(base, POOL_ROWS), :]
        return carry

    lax.fori_loop(0, n_steps, fill, 0)
    scale = scale_ref[...]

    def step(c, carry):
        base = pl.multiple_of(c * POOL_ROWS, POOL_ROWS)
        sums, centre = [], []
        for sub in range(0, POOL_ROWS, POOL_BAND_ROWS):
            win = vp_scr[pl.ds(base + sub, POOL_BAND_ROWS + 2 * POOL_HALO), :]
            sums.append(jnp.dot(band_ref[...], win, preferred_element_type=f32))
            centre.append(win[POOL_HALO:POOL_HALO + POOL_BAND_ROWS, :].astype(f32))
        pos = base + lax.broadcasted_iota(jnp.int32, (POOL_ROWS, 1), 0)
        cnt = (jnp.minimum(pos + half, seq) - jnp.maximum(pos - half, 0)).astype(f32)
        pooled = jnp.concatenate(sums, axis=0) / cnt - jnp.concatenate(centre, axis=0)
        mixed = jnp.dot(pooled.astype(bf16), mix_ref[...], preferred_element_type=f32)
        gate = gate_ref[pl.ds(base, POOL_ROWS), :].astype(f32)
        o_ref[pl.ds(base, POOL_ROWS), :] = (mixed * scale * _silu_of_twice(gate)).astype(o_ref.dtype)
        return carry

    lax.fori_loop(0, n_steps, step, 0, unroll=True)


def _pool_core(proj, mix_w, scale):
    bsz, seq, e2 = proj.shape
    e = e2 // 2
    ng = len(POOL_WINDOWS)
    gd = e // ng
    bands = _pool_bands()
    halves = jnp.asarray([w // 2 for w in POOL_WINDOWS], jnp.int32)
    grid_spec = pltpu.PrefetchScalarGridSpec(
        num_scalar_prefetch=1, grid=(bsz, ng),
        in_specs=[pl.BlockSpec((None, seq, gd), lambda b, g, hs: (b, 0, g)),
                  pl.BlockSpec((None, seq, gd), lambda b, g, hs: (b, 0, ng + g)),
                  pl.BlockSpec((None,) + bands.shape[1:], lambda b, g, hs: (g, 0, 0)),
                  pl.BlockSpec((None, gd, gd), lambda b, g, hs: (g, 0, 0)),
                  pl.BlockSpec((1, gd), lambda b, g, hs: (0, g))],
        out_specs=pl.BlockSpec((None, seq, gd), lambda b, g, hs: (b, 0, g)),
        scratch_shapes=[pltpu.VMEM((seq + 2 * POOL_HALO, gd), bf16)])
    return pl.pallas_call(
        _pool_kernel, grid_spec=grid_spec,
        out_shape=jax.ShapeDtypeStruct((bsz, seq, e), bf16),
        compiler_params=pltpu.CompilerParams(dimension_semantics=("arbitrary", "arbitrary"),
                                             vmem_limit_bytes=VMEM_LIMIT),
        name="pool_core")(halves, proj, proj, bands, mix_w, scale)


def _group_major(p):
    return p.reshape(2, SSD_GROUPS, HEADS_PER_GROUP).transpose(1, 0, 2).reshape(-1, 1)


def kernel(x, norm_w, ssd_w_in, ssd_conv_w, ssd_conv_b, ssd_dt_bias, ssd_a_log, ssd_d, ssd_norm_w, ssd_w_out, pool_w_in, pool_mix_w, pool_scale, pool_w_out, final_norm_w):
    bsz, seq, d = x.shape
    t = bsz * seq
    depth = norm_w.shape[0]
    e = ssd_w_out.shape[1]
    n_main = 2 * e + 2 * SSD_GROUPS * SSD_STATE
    h = x.reshape(t, d)
    for i in range(depth):
        j = i // 2
        nw = norm_w[i].reshape(1, d)
        final_nw = final_norm_w.reshape(1, d) if i == depth - 1 else None
        if i % 2 == 0:
            w = ssd_w_in[j]
            z_half = jnp.where(jnp.arange(n_main) < e, 0.5, 1.0).astype(f32)
            w_main = (w[:, :n_main] * z_half).astype(bf16)
            w_dt = w[:, n_main:]
            wdt_t = (w_dt.reshape(d, 2, SSD_GROUPS, HEADS_PER_GROUP).transpose(2, 1, 3, 0)
                     .reshape(-1, d).astype(bf16))
            u, dt_t = _norm_dt(h, nw, wdt_t)
            y = _ssd_mixer(u.reshape(bsz, seq, d), w_main, dt_t,
                           _group_major(ssd_dt_bias[j]), _group_major(ssd_a_log[j]),
                           0.5 * ssd_conv_w[j], 0.5 * ssd_conv_b[j].reshape(1, -1),
                           jnp.repeat(ssd_d[j], SSD_HEAD_DIM).reshape(1, e),
                           ssd_norm_w[j].reshape(1, e))
            h = _out_proj(y.reshape(t, e), ssd_w_out[j].astype(bf16), h, final_nw)
        else:
            gate_half = jnp.where(jnp.arange(2 * e) < e, 1.0, 0.5).astype(f32)
            proj = _in_proj(h, nw, (pool_w_in[j] * gate_half).astype(bf16))
            y = _pool_core(proj.reshape(bsz, seq, 2 * e), pool_mix_w[j].astype(bf16),
                           pool_scale[j].reshape(1, e))
            h = _out_proj(y.reshape(t, e), pool_w_out[j].astype(bf16), h, final_nw)
    return h.reshape(bsz, seq, d)
```

```python
import functools

import jax
import jax.numpy as jnp
import numpy as np
from jax import lax
from jax.experimental import pallas as pl
from jax.experimental.pallas import tpu as pltpu

f32 = jnp.float32
bf16 = jnp.bfloat16

EPS = 1e-6
LANES = 128
LOG2E = 1.4426950408889634
SSD_HEAD_DIM = 64
SSD_GROUPS = 8
SSD_STATE = 128
SSD_CONV = 5
HEADS_PER_GROUP = 4
GROUP_WIDTH = HEADS_PER_GROUP * SSD_HEAD_DIM
CHUNK = 128
HALO = 8
POOL_WINDOWS = (2, 4, 8, 16)
POOL_ROWS = 256
POOL_BAND_ROWS = 128
POOL_HALO = 16
PROJ_ROWS = 1024

VMEM_LIMIT = 48 * 1024 * 1024
SSD_VMEM_LIMIT = 58 * 1024 * 1024


def _silu_of_twice(half):
    return half + half * jnp.tanh(half)


def _rows_ahead(win, k):
    n = win.shape[0]
    if k % 8 == 0:
        return win[k:k + n - 2 * HALO, :]
    return pltpu.roll(win, n - k, axis=0)[0:n - 2 * HALO, :]


def _normed(h_ref, nw_ref):
    xv = h_ref[...]
    ms = jnp.mean(xv * xv, axis=-1, keepdims=True)
    return (xv * lax.rsqrt(ms + EPS) * nw_ref[...]).astype(bf16)


def _in_proj_kernel(h_ref, nw_ref, w_ref, o_ref, u_scr):
    @pl.when(pl.program_id(1) == 0)
    def _():
        u_scr[...] = _normed(h_ref, nw_ref)

    o_ref[...] = jnp.dot(u_scr[...], w_ref[...], preferred_element_type=f32).astype(o_ref.dtype)


def _in_proj(h, nw, w, *, tm=512, tn=4096):
    t, d = h.shape
    n = w.shape[1]
    return pl.pallas_call(
        _in_proj_kernel, grid=(t // tm, n // tn),
        in_specs=[pl.BlockSpec((tm, d), lambda i, j: (i, 0)),
                  pl.BlockSpec((1, d), lambda i, j: (0, 0)),
                  pl.BlockSpec((d, tn), lambda i, j: (0, j))],
        out_specs=pl.BlockSpec((tm, tn), lambda i, j: (i, j)),
        out_shape=jax.ShapeDtypeStruct((t, n), bf16),
        scratch_shapes=[pltpu.VMEM((tm, d), bf16)],
        compiler_params=pltpu.CompilerParams(dimension_semantics=("arbitrary", "arbitrary"),
                                             vmem_limit_bytes=VMEM_LIMIT),
        name="in_proj")(h, nw, w)


def _norm_dt_kernel(h_ref, nw_ref, wdt_ref, u_ref, dt_ref):
    u = _normed(h_ref, nw_ref)
    u_ref[...] = u
    dt_ref[...] = lax.dot_general(wdt_ref[...], u, (((1,), (1,)), ((), ())),
                                  preferred_element_type=f32)


def _norm_dt(h, nw, wdt_t, *, tm=1024):
    t, d = h.shape
    nd = wdt_t.shape[0]
    return pl.pallas_call(
        _norm_dt_kernel, grid=(t // tm,),
        in_specs=[pl.BlockSpec((tm, d), lambda i: (i, 0)),
                  pl.BlockSpec((1, d), lambda i: (0, 0)),
                  pl.BlockSpec((nd, d), lambda i: (0, 0))],
        out_specs=[pl.BlockSpec((tm, d), lambda i: (i, 0)), pl.BlockSpec((nd, tm), lambda i: (0, i))],
        out_shape=[jax.ShapeDtypeStruct((t, d), bf16), jax.ShapeDtypeStruct((nd, t), f32)],
        compiler_params=pltpu.CompilerParams(dimension_semantics=("arbitrary",),
                                             vmem_limit_bytes=VMEM_LIMIT),
        name="norm_dt")(h, nw, wdt_t)


def _out_proj_kernel(y_ref, w_ref, h_ref, o_ref):
    o_ref[...] = h_ref[...] + jnp.dot(y_ref[...], w_ref[...], preferred_element_type=f32)


def _out_proj_final_kernel(y_ref, w_ref, h_ref, nw_ref, o_ref):
    acc = h_ref[...] + jnp.dot(y_ref[...], w_ref[...], preferred_element_type=f32)
    ms = jnp.mean(acc * acc, axis=-1, keepdims=True)
    o_ref[...] = acc * lax.rsqrt(ms + EPS) * nw_ref[...]


def _out_proj(y, w, h, final_nw=None, *, tm=1024):
    t, e = y.shape
    d = w.shape[1]
    in_specs = [pl.BlockSpec((tm, e), lambda i: (i, 0)),
                pl.BlockSpec((e, d), lambda i: (0, 0)),
                pl.BlockSpec((tm, d), lambda i: (i, 0))]
    args = [y, w, h]
    body = _out_proj_kernel
    if final_nw is not None:
        in_specs.append(pl.BlockSpec((1, d), lambda i: (0, 0)))
        args.append(final_nw)
        body = _out_proj_final_kernel
    return pl.pallas_call(
        body, grid=(t // tm,), in_specs=in_specs,
        out_specs=pl.BlockSpec((tm, d), lambda i: (i, 0)),
        out_shape=jax.ShapeDtypeStruct((t, d), f32),
        compiler_params=pltpu.CompilerParams(dimension_semantics=("arbitrary",),
                                             vmem_limit_bytes=VMEM_LIMIT),
        name="out_proj")(*args)


def _split3(v):
    hi = v.astype(bf16).astype(f32)
    rem = v - hi
    mid = rem.astype(bf16).astype(f32)
    return hi, mid, rem - mid


_N_VALS = 24
_SPLIT_ROWS = 3 * _N_VALS
_MASKED = -1e30


_STRIP_BASE = 80
_STRIP_ROWS = 48


def _selectors():
    n_dir = 2 * HEADS_PER_GROUP
    sel_g = np.zeros((_STRIP_BASE, n_dir * CHUNK), np.float32)
    for j in range(n_dir):
        for p in range(3):
            sel_g[p * _N_VALS + j, j * CHUNK:(j + 1) * CHUNK] = 1.0 if j < HEADS_PER_GROUP else -1.0
    sels = [sel_g]
    for fwd_v, bwd_v in ((8, 12), (16, 20)):
        s = np.zeros((CHUNK, 2 * GROUP_WIDTH), np.float32)
        for a, v0 in enumerate((fwd_v, bwd_v)):
            for r in range(HEADS_PER_GROUP):
                lo = a * GROUP_WIDTH + r * SSD_HEAD_DIM
                for p in range(3):
                    s[p * _N_VALS + v0 + r, lo:lo + SSD_HEAD_DIM] = 1.0
        sels.append(s)
    return tuple(jnp.asarray(s, bf16) for s in sels)


def _ssd_kernel(u_ref, wz0_ref, wx0_ref, wb0_ref, wc0_ref, wz_ref, wx_ref, wb_ref, wc_ref,
                dt_ref, dtb_ref, alog_ref,
                cwx_ref, cwb_ref, cwc_ref, cbx_ref, cbb_ref, cbc_ref, dskip_ref, nw_ref,
                selg_ref, sels_ref, selc_ref,
                o_ref, xp_scr, xbc_scr, strip_scr, col_scr, y_scr, s_scr, edge_scr, hin_scr,
                exp_scr, cb_scr, proj_scr, gate_scr):
    seq = u_ref.shape[0]
    n_chunks = seq // CHUNK
    width = GROUP_WIDTH + 2 * SSD_STATE
    step = pl.program_id(0) * pl.num_programs(1) + pl.program_id(1)
    cur = step % 2
    nxt = 1 - cur

    @pl.when(step == 0)
    def _():
        w_first = jnp.concatenate([wz0_ref[...], wx0_ref[...], wb0_ref[...], wc0_ref[...]], axis=1)
        for r0 in range(0, seq, PROJ_ROWS):
            proj_scr[0, r0:r0 + PROJ_ROWS, :] = jnp.dot(
                u_ref[r0:r0 + PROJ_ROWS, :], w_first, preferred_element_type=f32).astype(bf16)

    def project_next(q, nt):
        rows = slice(q * PROJ_ROWS, (q + 1) * PROJ_ROWS)
        cols = slice(nt * GROUP_WIDTH, (nt + 1) * GROUP_WIDTH)
        if nt == 0:
            w_tile = wz_ref[...]
        elif nt == 1:
            w_tile = wx_ref[...]
        else:
            w_tile = jnp.concatenate([wb_ref[...], wc_ref[...]], axis=1)
        proj_scr[nxt, rows, cols] = jnp.dot(u_ref[rows, :], w_tile,
                                            preferred_element_type=f32).astype(bf16)

    xp_scr[0:HALO, :] = jnp.zeros((HALO, width), f32)
    xp_scr[seq + HALO:seq + 2 * HALO, :] = jnp.zeros((HALO, width), f32)

    for c in range(n_chunks):
        src_rows = slice(c * CHUNK, (c + 1) * CHUNK)
        rows = slice(c * CHUNK + HALO, (c + 1) * CHUNK + HALO)
        xp_scr[rows, :] = proj_scr[cur, src_rows, GROUP_WIDTH:GROUP_WIDTH + width].astype(f32)
        gate_scr[src_rows, :] = _silu_of_twice(proj_scr[cur, src_rows, 0:GROUP_WIDTH].astype(f32))

    n_dir = 2 * HEADS_PER_GROUP
    stack = lambda v: jnp.concatenate([v] * n_chunks, axis=0)
    raw = (jnp.concatenate([dt_ref[:, c * CHUNK:(c + 1) * CHUNK] for c in range(n_chunks)], axis=0)
           + stack(dtb_ref[...]))
    is_fwd = (lax.broadcasted_iota(jnp.int32, raw.shape, 0) & (n_dir - 1)) < HEADS_PER_GROUP
    dt = jnp.maximum(raw, 0.0) + jnp.log1p(jnp.exp(-jnp.abs(raw)))
    a2 = -(dt * stack(jnp.exp(alog_ref[...]) * LOG2E))
    tri = (lax.broadcasted_iota(jnp.int32, (CHUNK, CHUNK), 0)
           <= lax.broadcasted_iota(jnp.int32, (CHUNK, CHUNK), 1)).astype(bf16)
    sums = jnp.dot(jnp.concatenate(_split3(a2), axis=0).astype(bf16), tri, preferred_element_type=f32)
    n_rows = raw.shape[0]
    incl = sums[0:n_rows] + sums[n_rows:2 * n_rows] + sums[2 * n_rows:3 * n_rows]
    tot = incl[:, CHUNK - 1:CHUNK]
    cs = jnp.where(is_fwd, incl, incl - a2)
    log2_dt = jnp.maximum(jnp.log2(dt), -1e30)
    src_parts = _split3(jnp.where(is_fwd, log2_dt - cs, cs + log2_dt))
    to_end = jnp.exp2(tot - cs)
    from_start = jnp.exp2(cs)
    state_w = jnp.where(is_fwd, to_end, from_start) * dt
    carry_w = jnp.where(is_fwd, from_start, to_end)
    val_parts = list(zip(_split3(cs), _split3(state_w), _split3(carry_w)))

    own_block = (lax.broadcasted_iota(jnp.int32, (8, n_dir * CHUNK), 1) // CHUNK
                 == lax.broadcasted_iota(jnp.int32, (8, n_dir * CHUNK), 0))
    eye = (lax.broadcasted_iota(jnp.int32, (CHUNK, CHUNK), 0)
           == lax.broadcasted_iota(jnp.int32, (CHUNK, CHUNK), 1)).astype(bf16)
    ones_rows = jnp.ones((CHUNK - _SPLIT_ROWS, CHUNK), f32)
    zero_rows = jnp.zeros((8, n_dir * CHUNK), f32)
    contract1 = (((1,), (1,)), ((), ()))
    for c in range(n_chunks):
        own = slice(c * n_dir, (c + 1) * n_dir)
        rows = jnp.concatenate([v[own] for part in val_parts for v in part] + [ones_rows],
                               axis=0).astype(bf16)
        col_scr[c * CHUNK:(c + 1) * CHUNK, :] = lax.dot_general(
            eye, rows, contract1, preferred_element_type=f32).astype(bf16)
        strip = []
        for part in src_parts:
            tiled = jnp.concatenate([part[own]] * n_dir, axis=1)
            strip += [jnp.where(own_block, tiled, 0.0), zero_rows]
        strip_scr[c] = jnp.concatenate(strip, axis=0).astype(bf16)
    edges = jnp.concatenate(
        [col_scr[r0:r0 + 16, :] for c in range(n_chunks) for r0 in ((c + 1) * CHUNK - 16, c * CHUNK)],
        axis=0)
    edge_all = jnp.dot(edges, selc_ref[...], preferred_element_type=f32)
    for c in range(n_chunks):
        edge_scr[c] = edge_all[c * 32:(c + 1) * 32]

    conv_w = jnp.concatenate([cwx_ref[...], cwb_ref[...], cwc_ref[...]], axis=1)
    conv_b = jnp.concatenate([cbx_ref[...], cbb_ref[...], cbc_ref[...]], axis=1)
    dskip = dskip_ref[...]
    nw = nw_ref[...]
    lane_head = lax.broadcasted_iota(jnp.int32, (CHUNK, GROUP_WIDTH), 1) // SSD_HEAD_DIM
    contract0 = (((0,), (0,)), ((), ()))

    chunks_per_tile = PROJ_ROWS // CHUNK
    n_col_tiles = (GROUP_WIDTH + width) // GROUP_WIDTH
    tile_at = {}
    for q in range(seq // PROJ_ROWS):
        slots = [(q * chunks_per_tile + i, k) for i in range(chunks_per_tile) for k in (1, 3)]
        for nt in range(n_col_tiles):
            tile_at[slots[nt * len(slots) // n_col_tiles]] = (q, nt)

    def local_pass(c):
        base = c * CHUNK
        win = xp_scr[pl.ds(base, CHUNK + 2 * HALO), :]
        acc = jnp.broadcast_to(conv_b, (CHUNK, width))
        for k in range(SSD_CONV):
            acc = acc + _rows_ahead(win, HALO - SSD_CONV // 2 + k) * conv_w[k:k + 1, :]
            if (c, k) in tile_at:
                project_next(*tile_at[(c, k)])
        xbc = _silu_of_twice(acc)
        xbc_scr[pl.ds(base, CHUNK), :] = xbc.astype(bf16)
        xs = xbc[:, 0:GROUP_WIDTH]
        bm = xbc[:, GROUP_WIDTH:GROUP_WIDTH + SSD_STATE].astype(bf16)
        cm = xbc[:, GROUP_WIDTH + SSD_STATE:width].astype(bf16)
        y_scr[pl.ds(base, CHUNK), :] = xs * dskip
        col = col_scr[pl.ds(base, CHUNK), :]
        exp_scr[c] = jnp.dot(col, jnp.concatenate([selg_ref[...], strip_scr[c]], axis=0),
                             preferred_element_type=f32)
        cb_scr[c] = lax.dot_general(cm, bm, contract1, preferred_element_type=f32)
        state_w = jnp.dot(col, sels_ref[...], preferred_element_type=f32)
        weighted = jnp.concatenate([(xs * state_w[:, 0:GROUP_WIDTH]).astype(bf16),
                                    (xs * state_w[:, GROUP_WIDTH:]).astype(bf16)], axis=1)
        s_scr[c] = lax.dot_general(bm, weighted, contract0, preferred_element_type=f32)

    for c in range(n_chunks):
        local_pass(c)

    def recur(i, carry):
        hf, hb = carry
        cf = i
        cr = n_chunks - 1 - i
        hin_scr[cf, :, 0:GROUP_WIDTH] = hf.astype(bf16)
        hin_scr[cr, :, GROUP_WIDTH:2 * GROUP_WIDTH] = hb.astype(bf16)
        hf = edge_scr[cf][15:16, 0:GROUP_WIDTH] * hf + s_scr[cf][:, 0:GROUP_WIDTH]
        hb = edge_scr[cr][16:17, GROUP_WIDTH:2 * GROUP_WIDTH] * hb + s_scr[cr][:, GROUP_WIDTH:2 * GROUP_WIDTH]
        return hf, hb

    h0 = jnp.zeros((SSD_STATE, GROUP_WIDTH), f32)
    lax.fori_loop(0, n_chunks, recur, (h0, h0))


    li = lax.broadcasted_iota(jnp.int32, (CHUNK, CHUNK), 0)
    si = lax.broadcasted_iota(jnp.int32, (CHUNK, CHUNK), 1)
    lower = si <= li
    upper = si >= li

    def output_pass(c):
        base = c * CHUNK
        xbc = xbc_scr[pl.ds(base, CHUNK), :]
        xs = xbc[:, 0:GROUP_WIDTH]
        cm = xbc[:, GROUP_WIDTH + SSD_STATE:width]
        col = col_scr[pl.ds(base, CHUNK), :]
        carried = (jnp.dot(col, selc_ref[...], preferred_element_type=f32)
                   * jnp.dot(cm, hin_scr[c], preferred_element_type=f32))
        cb = cb_scr[c]
        g_heads, x_heads = [], []
        for r in range(HEADS_PER_GROUP):
            arg_f = exp_scr[c, :, r * CHUNK:(r + 1) * CHUNK]
            arg_b = exp_scr[c, :, (HEADS_PER_GROUP + r) * CHUNK:(HEADS_PER_GROUP + r + 1) * CHUNK]
            decay = (jnp.exp2(jnp.where(lower, arg_f, _MASKED))
                     + jnp.exp2(jnp.where(upper, arg_b, _MASKED)))
            g_heads.append((cb * decay).astype(bf16))
            x_heads.append(jnp.where(lane_head == r, xs, jnp.zeros_like(xs)))
        y = y_scr[pl.ds(base, CHUNK), :] + jnp.dot(
            jnp.concatenate(g_heads, axis=1), jnp.concatenate(x_heads, axis=0),
            preferred_element_type=f32)
        y = y + carried[:, 0:GROUP_WIDTH] + carried[:, GROUP_WIDTH:]
        y = y * gate_scr[pl.ds(base, CHUNK), :]
        ms = jnp.mean(y * y, axis=-1, keepdims=True)
        o_ref[pl.ds(base, CHUNK), :] = (y * lax.rsqrt(ms + EPS) * nw).astype(o_ref.dtype)

    for c in range(n_chunks):
        output_pass(c)


def _ssd_mixer(u, w_main, dt_t, dt_bias, a_log, conv_w, conv_b, dskip, norm_w):
    bsz, seq, d = u.shape
    e = SSD_GROUPS * GROUP_WIDTH
    width = GROUP_WIDTH + 2 * SSD_STATE
    n_proj = GROUP_WIDTH + width
    next_b = lambda b, g: jnp.minimum(b + (g + 1) // SSD_GROUPS, bsz - 1)
    next_g = lambda g: (g + 1) % SSD_GROUPS
    w_specs = lambda grp: [
        pl.BlockSpec((d, GROUP_WIDTH), lambda b, g: (0, grp(g))),
        pl.BlockSpec((d, GROUP_WIDTH), lambda b, g: (0, e // GROUP_WIDTH + grp(g))),
        pl.BlockSpec((d, SSD_STATE), lambda b, g: (0, 2 * e // SSD_STATE + grp(g))),
        pl.BlockSpec((d, SSD_STATE), lambda b, g: (0, 2 * e // SSD_STATE + SSD_GROUPS + grp(g))),
    ]
    in_specs = [pl.BlockSpec((None, seq, d), lambda b, g: (next_b(b, g), 0, 0))]
    in_specs += w_specs(lambda g: 0)
    in_specs += w_specs(next_g)
    in_specs += [
        pl.BlockSpec((8, seq), lambda b, g: (g, b)),
        pl.BlockSpec((8, 1), lambda b, g: (g, 0)),
        pl.BlockSpec((8, 1), lambda b, g: (g, 0)),
        pl.BlockSpec((SSD_CONV, GROUP_WIDTH), lambda b, g: (0, g)),
        pl.BlockSpec((SSD_CONV, SSD_STATE), lambda b, g: (0, e // SSD_STATE + g)),
        pl.BlockSpec((SSD_CONV, SSD_STATE), lambda b, g: (0, e // SSD_STATE + SSD_GROUPS + g)),
        pl.BlockSpec((1, GROUP_WIDTH), lambda b, g: (0, g)),
        pl.BlockSpec((1, SSD_STATE), lambda b, g: (0, e // SSD_STATE + g)),
        pl.BlockSpec((1, SSD_STATE), lambda b, g: (0, e // SSD_STATE + SSD_GROUPS + g)),
        pl.BlockSpec((1, GROUP_WIDTH), lambda b, g: (0, g)),
        pl.BlockSpec((1, GROUP_WIDTH), lambda b, g: (0, g)),
    ]
    selectors = _selectors()
    in_specs += [pl.BlockSpec(s.shape, lambda b, g: (0, 0)) for s in selectors]
    return pl.pallas_call(
        _ssd_kernel, grid=(bsz, SSD_GROUPS), in_specs=in_specs,
        out_specs=pl.BlockSpec((None, seq, GROUP_WIDTH), lambda b, g: (b, 0, g)),
        out_shape=jax.ShapeDtypeStruct((bsz, seq, e), bf16),
        scratch_shapes=[
            pltpu.VMEM((seq + 2 * HALO, width), f32),
            pltpu.VMEM((seq, width), bf16),
            pltpu.VMEM((seq // CHUNK, _STRIP_ROWS, 2 * HEADS_PER_GROUP * CHUNK), bf16),
            pltpu.VMEM((seq, CHUNK), bf16),
            pltpu.VMEM((seq, GROUP_WIDTH), f32),
            pltpu.VMEM((seq // CHUNK, SSD_STATE, 2 * GROUP_WIDTH), f32),
            pltpu.VMEM((seq // CHUNK, 32, 2 * GROUP_WIDTH), f32),
            pltpu.VMEM((seq // CHUNK, SSD_STATE, 2 * GROUP_WIDTH), bf16),
            pltpu.VMEM((seq // CHUNK, CHUNK, 2 * HEADS_PER_GROUP * CHUNK), f32),
            pltpu.VMEM((seq // CHUNK, CHUNK, CHUNK), f32),
            pltpu.VMEM((2, seq, n_proj), bf16),
            pltpu.VMEM((seq, GROUP_WIDTH), f32),
        ],
        compiler_params=pltpu.CompilerParams(dimension_semantics=("arbitrary", "arbitrary"),
                                             vmem_limit_bytes=SSD_VMEM_LIMIT),
        name="ssd_mixer")(u, *([w_main] * 8), dt_t, dt_bias, a_log,
                          conv_w, conv_w, conv_w, conv_b, conv_b, conv_b, dskip, norm_w,
                          *selectors)


def _pool_bands():
    m = np.arange(POOL_BAND_ROWS)[:, None] + POOL_HALO
    j = np.arange(POOL_BAND_ROWS + 2 * POOL_HALO)[None, :]
    bands = [((j >= m - w // 2) & (j < m + w // 2)).astype(np.float32) for w in POOL_WINDOWS]
    return jnp.asarray(np.stack(bands), bf16)


def _pool_kernel(half_ref, v_ref, gate_ref, band_ref, mix_ref, scale_ref, o_ref, vp_scr):
    seq, gd = v_ref.shape
    n_steps = seq // POOL_ROWS
    half = half_ref[pl.program_id(1)]

    vp_scr[0:POOL_HALO, :] = jnp.zeros((POOL_HALO, gd), bf16)
    vp_scr[seq + POOL_HALO:seq + 2 * POOL_HALO, :] = jnp.zeros((POOL_HALO, gd), bf16)

    def fill(c, carry):
        base = pl.multiple_of(c * POOL_ROWS, POOL_ROWS)
        vp_scr[pl.ds(base + POOL_HALO, POOL_ROWS), :] = v_ref[pl.ds(base, POOL_ROWS), :]
        return carry

    lax.fori_loop(0, n_steps, fill, 0)
    scale = scale_ref[...]

    def step(c, carry):
        base = pl.multiple_of(c * POOL_ROWS, POOL_ROWS)
        sums, centre = [], []
        for sub in range(0, POOL_ROWS, POOL_BAND_ROWS):
            win = vp_scr[pl.ds(base + sub, POOL_BAND_ROWS + 2 * POOL_HALO), :]
            sums.append(jnp.dot(band_ref[...], win, preferred_element_type=f32))
            centre.append(win[POOL_HALO:POOL_HALO + POOL_BAND_ROWS, :].astype(f32))
        pos = base + lax.broadcasted_iota(jnp.int32, (POOL_ROWS, 1), 0)
        cnt = (jnp.minimum(pos + half, seq) - jnp.maximum(pos - half, 0)).astype(f32)
        pooled = jnp.concatenate(sums, axis=0) / cnt - jnp.concatenate(centre, axis=0)
        mixed = jnp.dot(pooled.astype(bf16), mix_ref[...], preferred_element_type=f32)
        gate = gate_ref[pl.ds(base, POOL_ROWS), :].astype(f32)
        o_ref[pl.ds(base, POOL_ROWS), :] = (mixed * scale * _silu_of_twice(gate)).astype(o_ref.dtype)
        return carry

    lax.fori_loop(0, n_steps, step, 0, unroll=True)


def _pool_core(proj, mix_w, scale):
    bsz, seq, e2 = proj.shape
    e = e2 // 2
    ng = len(POOL_WINDOWS)
    gd = e // ng
    bands = _pool_bands()
    halves = jnp.asarray([w // 2 for w in POOL_WINDOWS], jnp.int32)
    grid_spec = pltpu.PrefetchScalarGridSpec(
        num_scalar_prefetch=1, grid=(bsz, ng),
        in_specs=[pl.BlockSpec((None, seq, gd), lambda b, g, hs: (b, 0, g)),
                  pl.BlockSpec((None, seq, gd), lambda b, g, hs: (b, 0, ng + g)),
                  pl.BlockSpec((None,) + bands.shape[1:], lambda b, g, hs: (g, 0, 0)),
                  pl.BlockSpec((None, gd, gd), lambda b, g, hs: (g, 0, 0)),
                  pl.BlockSpec((1, gd), lambda b, g, hs: (0, g))],
        out_specs=pl.BlockSpec((None, seq, gd), lambda b, g, hs: (b, 0, g)),
        scratch_shapes=[pltpu.VMEM((seq + 2 * POOL_HALO, gd), bf16)])
    return pl.pallas_call(
        _pool_kernel, grid_spec=grid_spec,
        out_shape=jax.ShapeDtypeStruct((bsz, seq, e), bf16),
        compiler_params=pltpu.CompilerParams(dimension_semantics=("arbitrary", "arbitrary"),
                                             vmem_limit_bytes=VMEM_LIMIT),
        name="pool_core")(halves, proj, proj, bands, mix_w, scale)


def _group_major(p):
    return p.reshape(2, SSD_GROUPS, HEADS_PER_GROUP).transpose(1, 0, 2).reshape(-1, 1)


def kernel(x, norm_w, ssd_w_in, ssd_conv_w, ssd_conv_b, ssd_dt_bias, ssd_a_log, ssd_d, ssd_norm_w, ssd_w_out, pool_w_in, pool_mix_w, pool_scale, pool_w_out, final_norm_w):
    bsz, seq, d = x.shape
    t = bsz * seq
    depth = norm_w.shape[0]
    e = ssd_w_out.shape[1]
    n_main = 2 * e + 2 * SSD_GROUPS * SSD_STATE
    h = x.reshape(t, d)
    for i in range(depth):
        j = i // 2
        nw = norm_w[i].reshape(1, d)
        final_nw = final_norm_w.reshape(1, d) if i == depth - 1 else None
        if i % 2 == 0:
            w = ssd_w_in[j]
            z_half = jnp.where(jnp.arange(n_main) < e, 0.5, 1.0).astype(f32)
            w_main = (w[:, :n_main] * z_half).astype(bf16)
            w_dt = w[:, n_main:]
            wdt_t = (w_dt.reshape(d, 2, SSD_GROUPS, HEADS_PER_GROUP).transpose(2, 1, 3, 0)
                     .reshape(-1, d).astype(bf16))
            u, dt_t = _norm_dt(h, nw, wdt_t)
            y = _ssd_mixer(u.reshape(bsz, seq, d), w_main, dt_t,
                           _group_major(ssd_dt_bias[j]), _group_major(ssd_a_log[j]),
                           0.5 * ssd_conv_w[j], 0.5 * ssd_conv_b[j].reshape(1, -1),
                           jnp.repeat(ssd_d[j], SSD_HEAD_DIM).reshape(1, e),
                           ssd_norm_w[j].reshape(1, e))
            h = _out_proj(y.reshape(t, e), ssd_w_out[j].astype(bf16), h, final_nw)
        else:
            gate_half = jnp.where(jnp.arange(2 * e) < e, 1.0, 0.5).astype(f32)
            proj = _in_proj(h, nw, (pool_w_in[j] * gate_half).astype(bf16))
            y = _pool_core(proj.reshape(bsz, seq, 2 * e), pool_mix_w[j].astype(bf16),
                           pool_scale[j].reshape(1, e))
            h = _out_proj(y.reshape(t, e), pool_w_out[j].astype(bf16), h, final_nw)
    return h.reshape(bsz, seq, d)
```

```python
import functools

import jax
import jax.numpy as jnp
import numpy as np
from jax import lax
from jax.experimental import pallas as pl
from jax.experimental.pallas import tpu as pltpu

f32 = jnp.float32
bf16 = jnp.bfloat16

EPS = 1e-6
LANES = 128
LOG2E = 1.4426950408889634
SSD_HEAD_DIM = 64
SSD_GROUPS = 8
SSD_STATE = 128
SSD_CONV = 5
HEADS_PER_GROUP = 4
GROUP_WIDTH = HEADS_PER_GROUP * SSD_HEAD_DIM
CHUNK = 128
HALO = 8
POOL_WINDOWS = (2, 4, 8, 16)
POOL_ROWS = 256
POOL_BAND_ROWS = 128
POOL_HALO = 16
PROJ_ROWS = 1024

VMEM_LIMIT = 48 * 1024 * 1024
SSD_VMEM_LIMIT = 58 * 1024 * 1024


def _silu_of_twice(half):
    return half + half * jnp.tanh(half)


def _rows_ahead(win, k):
    n = win.shape[0]
    if k % 8 == 0:
        return win[k:k + n - 2 * HALO, :]
    return pltpu.roll(win, n - k, axis=0)[0:n - 2 * HALO, :]


def _normed(h_ref, nw_ref):
    xv = h_ref[...]
    ms = jnp.mean(xv * xv, axis=-1, keepdims=True)
    return (xv * lax.rsqrt(ms + EPS) * nw_ref[...]).astype(bf16)


def _in_proj_kernel(h_ref, nw_ref, w_ref, o_ref, u_scr):
    @pl.when(pl.program_id(1) == 0)
    def _():
        u_scr[...] = _normed(h_ref, nw_ref)

    o_ref[...] = jnp.dot(u_scr[...], w_ref[...], preferred_element_type=f32).astype(o_ref.dtype)


def _in_proj(h, nw, w, *, tm=512, tn=4096):
    t, d = h.shape
    n = w.shape[1]
    return pl.pallas_call(
        _in_proj_kernel, grid=(t // tm, n // tn),
        in_specs=[pl.BlockSpec((tm, d), lambda i, j: (i, 0)),
                  pl.BlockSpec((1, d), lambda i, j: (0, 0)),
                  pl.BlockSpec((d, tn), lambda i, j: (0, j))],
        out_specs=pl.BlockSpec((tm, tn), lambda i, j: (i, j)),
        out_shape=jax.ShapeDtypeStruct((t, n), bf16),
        scratch_shapes=[pltpu.VMEM((tm, d), bf16)],
        compiler_params=pltpu.CompilerParams(dimension_semantics=("arbitrary", "arbitrary"),
                                             vmem_limit_bytes=VMEM_LIMIT),
        name="in_proj")(h, nw, w)


def _norm_dt_kernel(h_ref, nw_ref, wdt_ref, u_ref, dt_ref):
    u = _normed(h_ref, nw_ref)
    u_ref[...] = u
    dt_ref[...] = lax.dot_general(wdt_ref[...], u, (((1,), (1,)), ((), ())),
                                  preferred_element_type=f32)


def _norm_dt(h, nw, wdt_t, *, tm=1024):
    t, d = h.shape
    nd = wdt_t.shape[0]
    return pl.pallas_call(
        _norm_dt_kernel, grid=(t // tm,),
        in_specs=[pl.BlockSpec((tm, d), lambda i: (i, 0)),
                  pl.BlockSpec((1, d), lambda i: (0, 0)),
                  pl.BlockSpec((nd, d), lambda i: (0, 0))],
        out_specs=[pl.BlockSpec((tm, d), lambda i: (i, 0)), pl.BlockSpec((nd, tm), lambda i: (0, i))],
        out_shape=[jax.ShapeDtypeStruct((t, d), bf16), jax.ShapeDtypeStruct((nd, t), f32)],
        compiler_params=pltpu.CompilerParams(dimension_semantics=("arbitrary",),
                                             vmem_limit_bytes=VMEM_LIMIT),
        name="norm_dt")(h, nw, wdt_t)


def _out_proj_kernel(y_ref, w_ref, h_ref, o_ref):
    o_ref[...] = h_ref[...] + jnp.dot(y_ref[...], w_ref[...], preferred_element_type=f32)


def _out_proj_final_kernel(y_ref, w_ref, h_ref, nw_ref, o_ref):
    acc = h_ref[...] + jnp.dot(y_ref[...], w_ref[...], preferred_element_type=f32)
    ms = jnp.mean(acc * acc, axis=-1, keepdims=True)
    o_ref[...] = acc * lax.rsqrt(ms + EPS) * nw_ref[...]


def _out_proj_norm_dt_kernel(y_ref, w_ref, h_ref, nw_ref, wdt_ref, o_ref, u_ref, dt_ref):
    acc = h_ref[...] + jnp.dot(y_ref[...], w_ref[...], preferred_element_type=f32)
    o_ref[...] = acc
    ms = jnp.mean(acc * acc, axis=-1, keepdims=True)
    u = (acc * lax.rsqrt(ms + EPS) * nw_ref[...]).astype(bf16)
    u_ref[...] = u
    dt_ref[...] = lax.dot_general(wdt_ref[...], u, (((1,), (1,)), ((), ())),
                                  preferred_element_type=f32)


def _out_proj_norm_dt(y, w, h, next_nw, next_wdt_t, *, tm=1024):
    t, e = y.shape
    d = w.shape[1]
    nd = next_wdt_t.shape[0]
    return pl.pallas_call(
        _out_proj_norm_dt_kernel, grid=(t // tm,),
        in_specs=[pl.BlockSpec((tm, e), lambda i: (i, 0)),
                  pl.BlockSpec((e, d), lambda i: (0, 0)),
                  pl.BlockSpec((tm, d), lambda i: (i, 0)),
                  pl.BlockSpec((1, d), lambda i: (0, 0)),
                  pl.BlockSpec((nd, d), lambda i: (0, 0))],
        out_specs=[pl.BlockSpec((tm, d), lambda i: (i, 0)),
                   pl.BlockSpec((tm, d), lambda i: (i, 0)),
                   pl.BlockSpec((nd, tm), lambda i: (0, i))],
        out_shape=[jax.ShapeDtypeStruct((t, d), f32), jax.ShapeDtypeStruct((t, d), bf16),
                   jax.ShapeDtypeStruct((nd, t), f32)],
        compiler_params=pltpu.CompilerParams(dimension_semantics=("arbitrary",),
                                             vmem_limit_bytes=VMEM_LIMIT),
        name="out_proj_norm_dt")(y, w, h, next_nw, next_wdt_t)


def _out_proj(y, w, h, final_nw=None, *, tm=1024):
    t, e = y.shape
    d = w.shape[1]
    in_specs = [pl.BlockSpec((tm, e), lambda i: (i, 0)),
                pl.BlockSpec((e, d), lambda i: (0, 0)),
                pl.BlockSpec((tm, d), lambda i: (i, 0))]
    args = [y, w, h]
    body = _out_proj_kernel
    if final_nw is not None:
        in_specs.append(pl.BlockSpec((1, d), lambda i: (0, 0)))
        args.append(final_nw)
        body = _out_proj_final_kernel
    return pl.pallas_call(
        body, grid=(t // tm,), in_specs=in_specs,
        out_specs=pl.BlockSpec((tm, d), lambda i: (i, 0)),
        out_shape=jax.ShapeDtypeStruct((t, d), f32),
        compiler_params=pltpu.CompilerParams(dimension_semantics=("arbitrary",),
                                             vmem_limit_bytes=VMEM_LIMIT),
        name="out_proj")(*args)


def _split3(v):
    hi = v.astype(bf16).astype(f32)
    rem = v - hi
    mid = rem.astype(bf16).astype(f32)
    return hi, mid, rem - mid


_N_VALS = 24
_SPLIT_ROWS = 3 * _N_VALS
_MASKED = -1e30


_STRIP_BASE = 80
_STRIP_ROWS = 48


def _selectors():
    n_dir = 2 * HEADS_PER_GROUP
    sel_g = np.zeros((_STRIP_BASE, n_dir * CHUNK), np.float32)
    for j in range(n_dir):
        for p in range(3):
            sel_g[p * _N_VALS + j, j * CHUNK:(j + 1) * CHUNK] = 1.0 if j < HEADS_PER_GROUP else -1.0
    sels = [sel_g]
    for fwd_v, bwd_v in ((8, 12), (16, 20)):
        s = np.zeros((CHUNK, 2 * GROUP_WIDTH), np.float32)
        for a, v0 in enumerate((fwd_v, bwd_v)):
            for r in range(HEADS_PER_GROUP):
                lo = a * GROUP_WIDTH + r * SSD_HEAD_DIM
                for p in range(3):
                    s[p * _N_VALS + v0 + r, lo:lo + SSD_HEAD_DIM] = 1.0
        sels.append(s)
    return tuple(jnp.asarray(s, bf16) for s in sels)


def _ssd_kernel(u_ref, wz0_ref, wx0_ref, wb0_ref, wc0_ref, wz_ref, wx_ref, wb_ref, wc_ref,
                dt_ref, dtb_ref, alog_ref,
                cwx_ref, cwb_ref, cwc_ref, cbx_ref, cbb_ref, cbc_ref, dskip_ref, nw_ref,
                selg_ref, sels_ref, selc_ref,
                o_ref, xp_scr, xbc_scr, strip_scr, col_scr, y_scr, s_scr, edge_scr, hin_scr,
                exp_scr, cb_scr, proj_scr, gate_scr):
    seq = u_ref.shape[0]
    n_chunks = seq // CHUNK
    width = GROUP_WIDTH + 2 * SSD_STATE
    step = pl.program_id(0) * pl.num_programs(1) + pl.program_id(1)
    cur = step % 2
    nxt = 1 - cur

    @pl.when(step == 0)
    def _():
        w_first = jnp.concatenate([wz0_ref[...], wx0_ref[...], wb0_ref[...], wc0_ref[...]], axis=1)
        for r0 in range(0, seq, PROJ_ROWS):
            proj_scr[0, r0:r0 + PROJ_ROWS, :] = jnp.dot(
                u_ref[r0:r0 + PROJ_ROWS, :], w_first, preferred_element_type=f32).astype(bf16)

    def project_next(q, nt):
        rows = slice(q * PROJ_ROWS, (q + 1) * PROJ_ROWS)
        cols = slice(nt * GROUP_WIDTH, (nt + 1) * GROUP_WIDTH)
        if nt == 0:
            w_tile = wz_ref[...]
        elif nt == 1:
            w_tile = wx_ref[...]
        else:
            w_tile = jnp.concatenate([wb_ref[...], wc_ref[...]], axis=1)
        proj_scr[nxt, rows, cols] = jnp.dot(u_ref[rows, :], w_tile,
                                            preferred_element_type=f32).astype(bf16)

    xp_scr[0:HALO, :] = jnp.zeros((HALO, width), f32)
    xp_scr[seq + HALO:seq + 2 * HALO, :] = jnp.zeros((HALO, width), f32)

    for c in range(n_chunks):
        src_rows = slice(c * CHUNK, (c + 1) * CHUNK)
        rows = slice(c * CHUNK + HALO, (c + 1) * CHUNK + HALO)
        xp_scr[rows, :] = proj_scr[cur, src_rows, GROUP_WIDTH:GROUP_WIDTH + width].astype(f32)
        gate_scr[src_rows, :] = _silu_of_twice(proj_scr[cur, src_rows, 0:GROUP_WIDTH].astype(f32))

    n_dir = 2 * HEADS_PER_GROUP
    stack = lambda v: jnp.concatenate([v] * n_chunks, axis=0)
    raw = (jnp.concatenate([dt_ref[:, c * CHUNK:(c + 1) * CHUNK] for c in range(n_chunks)], axis=0)
           + stack(dtb_ref[...]))
    is_fwd = (lax.broadcasted_iota(jnp.int32, raw.shape, 0) & (n_dir - 1)) < HEADS_PER_GROUP
    dt = jnp.maximum(raw, 0.0) + jnp.log1p(jnp.exp(-jnp.abs(raw)))
    a2 = -(dt * stack(jnp.exp(alog_ref[...]) * LOG2E))
    tri = (lax.broadcasted_iota(jnp.int32, (CHUNK, CHUNK), 0)
           <= lax.broadcasted_iota(jnp.int32, (CHUNK, CHUNK), 1)).astype(bf16)
    sums = jnp.dot(jnp.concatenate(_split3(a2), axis=0).astype(bf16), tri, preferred_element_type=f32)
    n_rows = raw.shape[0]
    incl = sums[0:n_rows] + sums[n_rows:2 * n_rows] + sums[2 * n_rows:3 * n_rows]
    tot = incl[:, CHUNK - 1:CHUNK]
    cs = jnp.where(is_fwd, incl, incl - a2)
    log2_dt = jnp.maximum(jnp.log2(dt), -1e30)
    src_parts = _split3(jnp.where(is_fwd, log2_dt - cs, cs + log2_dt))
    to_end = jnp.exp2(tot - cs)
    from_start = jnp.exp2(cs)
    state_w = jnp.where(is_fwd, to_end, from_start) * dt
    carry_w = jnp.where(is_fwd, from_start, to_end)
    val_parts = list(zip(_split3(cs), _split3(state_w), _split3(carry_w)))

    own_block = (lax.broadcasted_iota(jnp.int32, (8, n_dir * CHUNK), 1) // CHUNK
                 == lax.broadcasted_iota(jnp.int32, (8, n_dir * CHUNK), 0))
    eye = (lax.broadcasted_iota(jnp.int32, (CHUNK, CHUNK), 0)
           == lax.broadcasted_iota(jnp.int32, (CHUNK, CHUNK), 1)).astype(bf16)
    ones_rows = jnp.ones((CHUNK - _SPLIT_ROWS, CHUNK), f32)
    zero_rows = jnp.zeros((8, n_dir * CHUNK), f32)
    contract1 = (((1,), (1,)), ((), ()))
    for c in range(n_chunks):
        own = slice(c * n_dir, (c + 1) * n_dir)
        rows = jnp.concatenate([v[own] for part in val_parts for v in part] + [ones_rows],
                               axis=0).astype(bf16)
        col_scr[c * CHUNK:(c + 1) * CHUNK, :] = lax.dot_general(
            eye, rows, contract1, preferred_element_type=f32).astype(bf16)
        strip = []
        for part in src_parts:
            tiled = jnp.concatenate([part[own]] * n_dir, axis=1)
            strip += [jnp.where(own_block, tiled, 0.0), zero_rows]
        strip_scr[c] = jnp.concatenate(strip, axis=0).astype(bf16)
    edges = jnp.concatenate(
        [col_scr[r0:r0 + 16, :] for c in range(n_chunks) for r0 in ((c + 1) * CHUNK - 16, c * CHUNK)],
        axis=0)
    edge_all = jnp.dot(edges, selc_ref[...], preferred_element_type=f32)
    for c in range(n_chunks):
        edge_scr[c] = edge_all[c * 32:(c + 1) * 32]

    conv_w = jnp.concatenate([cwx_ref[...], cwb_ref[...], cwc_ref[...]], axis=1)
    conv_b = jnp.concatenate([cbx_ref[...], cbb_ref[...], cbc_ref[...]], axis=1)
    dskip = dskip_ref[...]
    nw = nw_ref[...]
    lane_head = lax.broadcasted_iota(jnp.int32, (CHUNK, GROUP_WIDTH), 1) // SSD_HEAD_DIM
    contract0 = (((0,), (0,)), ((), ()))

    chunks_per_tile = PROJ_ROWS // CHUNK
    n_col_tiles = (GROUP_WIDTH + width) // GROUP_WIDTH
    tile_at = {}
    for q in range(seq // PROJ_ROWS):
        slots = [(q * chunks_per_tile + i, k) for i in range(chunks_per_tile) for k in (1, 3)]
        for nt in range(n_col_tiles):
            tile_at[slots[nt * len(slots) // n_col_tiles]] = (q, nt)

    def local_pass(c):
        base = c * CHUNK
        win = xp_scr[pl.ds(base, CHUNK + 2 * HALO), :]
        acc = jnp.broadcast_to(conv_b, (CHUNK, width))
        for k in range(SSD_CONV):
            acc = acc + _rows_ahead(win, HALO - SSD_CONV // 2 + k) * conv_w[k:k + 1, :]
            if (c, k) in tile_at:
                project_next(*tile_at[(c, k)])
        xbc = _silu_of_twice(acc)
        xbc_scr[pl.ds(base, CHUNK), :] = xbc.astype(bf16)
        xs = xbc[:, 0:GROUP_WIDTH]
        bm = xbc[:, GROUP_WIDTH:GROUP_WIDTH + SSD_STATE].astype(bf16)
        cm = xbc[:, GROUP_WIDTH + SSD_STATE:width].astype(bf16)
        y_scr[pl.ds(base, CHUNK), :] = xs * dskip
        col = col_scr[pl.ds(base, CHUNK), :]
        exp_scr[c] = jnp.dot(col, jnp.concatenate([selg_ref[...], strip_scr[c]], axis=0),
                             preferred_element_type=f32)
        cb_scr[c] = lax.dot_general(cm, bm, contract1, preferred_element_type=f32)
        state_w = jnp.dot(col, sels_ref[...], preferred_element_type=f32)
        weighted = jnp.concatenate([(xs * state_w[:, 0:GROUP_WIDTH]).astype(bf16),
                                    (xs * state_w[:, GROUP_WIDTH:]).astype(bf16)], axis=1)
        s_scr[c] = lax.dot_general(bm, weighted, contract0, preferred_element_type=f32)

    for c in range(n_chunks):
        local_pass(c)

    def recur(i, carry):
        hf, hb = carry
        cf = i
        cr = n_chunks - 1 - i
        hin_scr[cf, :, 0:GROUP_WIDTH] = hf.astype(bf16)
        hin_scr[cr, :, GROUP_WIDTH:2 * GROUP_WIDTH] = hb.astype(bf16)
        hf = edge_scr[cf][15:16, 0:GROUP_WIDTH] * hf + s_scr[cf][:, 0:GROUP_WIDTH]
        hb = edge_scr[cr][16:17, GROUP_WIDTH:2 * GROUP_WIDTH] * hb + s_scr[cr][:, GROUP_WIDTH:2 * GROUP_WIDTH]
        return hf, hb

    h0 = jnp.zeros((SSD_STATE, GROUP_WIDTH), f32)
    lax.fori_loop(0, n_chunks, recur, (h0, h0))


    li = lax.broadcasted_iota(jnp.int32, (CHUNK, CHUNK), 0)
    si = lax.broadcasted_iota(jnp.int32, (CHUNK, CHUNK), 1)
    lower = si <= li
    upper = si >= li

    def output_pass(c):
        base = c * CHUNK
        xbc = xbc_scr[pl.ds(base, CHUNK), :]
        xs = xbc[:, 0:GROUP_WIDTH]
        cm = xbc[:, GROUP_WIDTH + SSD_STATE:width]
        col = col_scr[pl.ds(base, CHUNK), :]
        carried = (jnp.dot(col, selc_ref[...], preferred_element_type=f32)
                   * jnp.dot(cm, hin_scr[c], preferred_element_type=f32))
        cb = cb_scr[c]
        g_heads, x_heads = [], []
        for r in range(HEADS_PER_GROUP):
            arg_f = exp_scr[c, :, r * CHUNK:(r + 1) * CHUNK]
            arg_b = exp_scr[c, :, (HEADS_PER_GROUP + r) * CHUNK:(HEADS_PER_GROUP + r + 1) * CHUNK]
            decay = (jnp.exp2(jnp.where(lower, arg_f, _MASKED))
                     + jnp.exp2(jnp.where(upper, arg_b, _MASKED)))
            g_heads.append((cb * decay).astype(bf16))
            x_heads.append(jnp.where(lane_head == r, xs, jnp.zeros_like(xs)))
        y = y_scr[pl.ds(base, CHUNK), :] + jnp.dot(
            jnp.concatenate(g_heads, axis=1), jnp.concatenate(x_heads, axis=0),
            preferred_element_type=f32)
        y = y + carried[:, 0:GROUP_WIDTH] + carried[:, GROUP_WIDTH:]
        y = y * gate_scr[pl.ds(base, CHUNK), :]
        ms = jnp.mean(y * y, axis=-1, keepdims=True)
        o_ref[pl.ds(base, CHUNK), :] = (y * lax.rsqrt(ms + EPS) * nw).astype(o_ref.dtype)

    for c in range(n_chunks):
        output_pass(c)


def _ssd_mixer(u, w_main, dt_t, dt_bias, a_log, conv_w, conv_b, dskip, norm_w):
    bsz, seq, d = u.shape
    e = SSD_GROUPS * GROUP_WIDTH
    width = GROUP_WIDTH + 2 * SSD_STATE
    n_proj = GROUP_WIDTH + width
    next_b = lambda b, g: jnp.minimum(b + (g + 1) // SSD_GROUPS, bsz - 1)
    next_g = lambda g: (g + 1) % SSD_GROUPS
    w_specs = lambda grp: [
        pl.BlockSpec((d, GROUP_WIDTH), lambda b, g: (0, grp(g))),
        pl.BlockSpec((d, GROUP_WIDTH), lambda b, g: (0, e // GROUP_WIDTH + grp(g))),
        pl.BlockSpec((d, SSD_STATE), lambda b, g: (0, 2 * e // SSD_STATE + grp(g))),
        pl.BlockSpec((d, SSD_STATE), lambda b, g: (0, 2 * e // SSD_STATE + SSD_GROUPS + grp(g))),
    ]
    in_specs = [pl.BlockSpec((None, seq, d), lambda b, g: (next_b(b, g), 0, 0))]
    in_specs += w_specs(lambda g: 0)
    in_specs += w_specs(next_g)
    in_specs += [
        pl.BlockSpec((8, seq), lambda b, g: (g, b)),
        pl.BlockSpec((8, 1), lambda b, g: (g, 0)),
        pl.BlockSpec((8, 1), lambda b, g: (g, 0)),
        pl.BlockSpec((SSD_CONV, GROUP_WIDTH), lambda b, g: (0, g)),
        pl.BlockSpec((SSD_CONV, SSD_STATE), lambda b, g: (0, e // SSD_STATE + g)),
        pl.BlockSpec((SSD_CONV, SSD_STATE), lambda b, g: (0, e // SSD_STATE + SSD_GROUPS + g)),
        pl.BlockSpec((1, GROUP_WIDTH), lambda b, g: (0, g)),
        pl.BlockSpec((1, SSD_STATE), lambda b, g: (0, e // SSD_STATE + g)),
        pl.BlockSpec((1, SSD_STATE), lambda b, g: (0, e // SSD_STATE + SSD_GROUPS + g)),
        pl.BlockSpec((1, GROUP_WIDTH), lambda b, g: (0, g)),
        pl.BlockSpec((1, GROUP_WIDTH), lambda b, g: (0, g)),
    ]
    selectors = _selectors()
    in_specs += [pl.BlockSpec(s.shape, lambda b, g: (0, 0)) for s in selectors]
    return pl.pallas_call(
        _ssd_kernel, grid=(bsz, SSD_GROUPS), in_specs=in_specs,
        out_specs=pl.BlockSpec((None, seq, GROUP_WIDTH), lambda b, g: (b, 0, g)),
        out_shape=jax.ShapeDtypeStruct((bsz, seq, e), bf16),
        scratch_shapes=[
            pltpu.VMEM((seq + 2 * HALO, width), f32),
            pltpu.VMEM((seq, width), bf16),
            pltpu.VMEM((seq // CHUNK, _STRIP_ROWS, 2 * HEADS_PER_GROUP * CHUNK), bf16),
            pltpu.VMEM((seq, CHUNK), bf16),
            pltpu.VMEM((seq, GROUP_WIDTH), f32),
            pltpu.VMEM((seq // CHUNK, SSD_STATE, 2 * GROUP_WIDTH), f32),
            pltpu.VMEM((seq // CHUNK, 32, 2 * GROUP_WIDTH), f32),
            pltpu.VMEM((seq // CHUNK, SSD_STATE, 2 * GROUP_WIDTH), bf16),
            pltpu.VMEM((seq // CHUNK, CHUNK, 2 * HEADS_PER_GROUP * CHUNK), f32),
            pltpu.VMEM((seq // CHUNK, CHUNK, CHUNK), f32),
            pltpu.VMEM((2, seq, n_proj), bf16),
            pltpu.VMEM((seq, GROUP_WIDTH), f32),
        ],
        compiler_params=pltpu.CompilerParams(dimension_semantics=("arbitrary", "arbitrary"),
                                             vmem_limit_bytes=SSD_VMEM_LIMIT),
        name="ssd_mixer")(u, *([w_main] * 8), dt_t, dt_bias, a_log,
                          conv_w, conv_w, conv_w, conv_b, conv_b, conv_b, dskip, norm_w,
                          *selectors)


def _pool_bands():
    m = np.arange(POOL_BAND_ROWS)[:, None] + POOL_HALO
    j = np.arange(POOL_BAND_ROWS + 2 * POOL_HALO)[None, :]
    bands = [((j >= m - w // 2) & (j < m + w // 2)).astype(np.float32) for w in POOL_WINDOWS]
    return jnp.asarray(np.stack(bands), bf16)


def _pool_kernel(half_ref, v_ref, gate_ref, band_ref, mix_ref, scale_ref, o_ref, vp_scr):
    seq, gd = v_ref.shape
    n_steps = seq // POOL_ROWS
    half = half_ref[pl.program_id(1)]

    vp_scr[0:POOL_HALO, :] = jnp.zeros((POOL_HALO, gd), bf16)
    vp_scr[seq + POOL_HALO:seq + 2 * POOL_HALO, :] = jnp.zeros((POOL_HALO, gd), bf16)

    def fill(c, carry):
        base = pl.multiple_of(c * POOL_ROWS, POOL_ROWS)
        vp_scr[pl.ds(base + POOL_HALO, POOL_ROWS), :] = v_ref[pl.ds(base, POOL_ROWS), :]
        return carry

    lax.fori_loop(0, n_steps, fill, 0)
    scale = scale_ref[...]

    def step(c, carry):
        base = pl.multiple_of(c * POOL_ROWS, POOL_ROWS)
        sums, centre = [], []
        for sub in range(0, POOL_ROWS, POOL_BAND_ROWS):
            win = vp_scr[pl.ds(base + sub, POOL_BAND_ROWS + 2 * POOL_HALO), :]
            sums.append(jnp.dot(band_ref[...], win, preferred_element_type=f32))
            centre.append(win[POOL_HALO:POOL_HALO + POOL_BAND_ROWS, :].astype(f32))
        pos = base + lax.broadcasted_iota(jnp.int32, (POOL_ROWS, 1), 0)
        cnt = (jnp.minimum(pos + half, seq) - jnp.maximum(pos - half, 0)).astype(f32)
        pooled = jnp.concatenate(sums, axis=0) / cnt - jnp.concatenate(centre, axis=0)
        mixed = jnp.dot(pooled.astype(bf16), mix_ref[...], preferred_element_type=f32)
        gate = gate_ref[pl.ds(base, POOL_ROWS), :].astype(f32)
        o_ref[pl.ds(base, POOL_ROWS), :] = (mixed * scale * _silu_of_twice(gate)).astype(o_ref.dtype)
        return carry

    lax.fori_loop(0, n_steps, step, 0, unroll=True)


def _pool_core(proj, mix_w, scale):
    bsz, seq, e2 = proj.shape
    e = e2 // 2
    ng = len(POOL_WINDOWS)
    gd = e // ng
    bands = _pool_bands()
    halves = jnp.asarray([w // 2 for w in POOL_WINDOWS], jnp.int32)
    grid_spec = pltpu.PrefetchScalarGridSpec(
        num_scalar_prefetch=1, grid=(bsz, ng),
        in_specs=[pl.BlockSpec((None, seq, gd), lambda b, g, hs: (b, 0, g)),
                  pl.BlockSpec((None, seq, gd), lambda b, g, hs: (b, 0, ng + g)),
                  pl.BlockSpec((None,) + bands.shape[1:], lambda b, g, hs: (g, 0, 0)),
                  pl.BlockSpec((None, gd, gd), lambda b, g, hs: (g, 0, 0)),
                  pl.BlockSpec((1, gd), lambda b, g, hs: (0, g))],
        out_specs=pl.BlockSpec((None, seq, gd), lambda b, g, hs: (b, 0, g)),
        scratch_shapes=[pltpu.VMEM((seq + 2 * POOL_HALO, gd), bf16)])
    return pl.pallas_call(
        _pool_kernel, grid_spec=grid_spec,
        out_shape=jax.ShapeDtypeStruct((bsz, seq, e), bf16),
        compiler_params=pltpu.CompilerParams(dimension_semantics=("arbitrary", "arbitrary"),
                                             vmem_limit_bytes=VMEM_LIMIT),
        name="pool_core")(halves, proj, proj, bands, mix_w, scale)


def _group_major(p):
    return p.reshape(2, SSD_GROUPS, HEADS_PER_GROUP).transpose(1, 0, 2).reshape(-1, 1)


def kernel(x, norm_w, ssd_w_in, ssd_conv_w, ssd_conv_b, ssd_dt_bias, ssd_a_log, ssd_d, ssd_norm_w, ssd_w_out, pool_w_in, pool_mix_w, pool_scale, pool_w_out, final_norm_w):
    bsz, seq, d = x.shape
    t = bsz * seq
    depth = norm_w.shape[0]
    e = ssd_w_out.shape[1]
    n_main = 2 * e + 2 * SSD_GROUPS * SSD_STATE
    h = x.reshape(t, d)

    def dt_weights(layer):
        w_dt = ssd_w_in[layer][:, n_main:]
        return (w_dt.reshape(d, 2, SSD_GROUPS, HEADS_PER_GROUP).transpose(2, 1, 3, 0)
                .reshape(-1, d).astype(bf16))

    normed = None
    for i in range(depth):
        j = i // 2
        nw = norm_w[i].reshape(1, d)
        final_nw = final_norm_w.reshape(1, d) if i == depth - 1 else None
        if i % 2 == 0:
            w = ssd_w_in[j]
            z_half = jnp.where(jnp.arange(n_main) < e, 0.5, 1.0).astype(f32)
            w_main = (w[:, :n_main] * z_half).astype(bf16)
            u, dt_t = normed if normed is not None else _norm_dt(h, nw, dt_weights(j))
            normed = None
            y = _ssd_mixer(u.reshape(bsz, seq, d), w_main, dt_t,
                           _group_major(ssd_dt_bias[j]), _group_major(ssd_a_log[j]),
                           0.5 * ssd_conv_w[j], 0.5 * ssd_conv_b[j].reshape(1, -1),
                           jnp.repeat(ssd_d[j], SSD_HEAD_DIM).reshape(1, e),
                           ssd_norm_w[j].reshape(1, e))
            h = _out_proj(y.reshape(t, e), ssd_w_out[j].astype(bf16), h, final_nw)
        else:
            gate_half = jnp.where(jnp.arange(2 * e) < e, 1.0, 0.5).astype(f32)
            proj = _in_proj(h, nw, (pool_w_in[j] * gate_half).astype(bf16))
            y = _pool_core(proj.reshape(bsz, seq, 2 * e), pool_mix_w[j].astype(bf16),
                           pool_scale[j].reshape(1, e))
            if i + 1 < depth:
                h, u_next, dt_next = _out_proj_norm_dt(
                    y.reshape(t, e), pool_w_out[j].astype(bf16), h,
                    norm_w[i + 1].reshape(1, d), dt_weights((i + 1) // 2))
                normed = (u_next, dt_next)
            else:
                h = _out_proj(y.reshape(t, e), pool_w_out[j].astype(bf16), h, final_nw)
    return h.reshape(bsz, seq, d)
```
